```python
import math
import jax, jax.numpy as jnp
from jax import lax
import numpy as np

D_MODEL = 1024
BATCH = 2
SEQ = 8192
DEPTH = 1

HEAD_DIM = 64
N_HEADS = (D_MODEL // 2) // HEAD_DIM
N_KV_HEADS = max(1, N_HEADS // 4)
GROUP = N_HEADS // N_KV_HEADS
ATTN_WIDTH = N_HEADS * HEAD_DIM
KV_WIDTH = N_KV_HEADS * HEAD_DIM
WINDOW = 128
BLOCK = 128
POOL_WINDOWS = (2, 4, 8, 16)
N_POOL_GROUPS = len(POOL_WINDOWS)
POOL_WIDTH = D_MODEL // 2
POOL_GROUP_DIM = POOL_WIDTH // N_POOL_GROUPS
D_FF = 4 * D_MODEL
IN_WIDTH = POOL_WIDTH + ATTN_WIDTH + 2 * KV_WIDTH + 2 * D_MODEL
RMS_EPS = 1e-5
NEG_INF = -1e30
ALIBI_SLOPES = np.array([2.0 ** (-8.0 * (h + 1) / N_HEADS) for h in range(N_HEADS)], dtype=np.float32)

kernel_name = "hybrid_pool_swa_gated_block"


def rms_norm(x, g):
    xf = x.astype(jnp.float32)
    y = xf * lax.rsqrt(jnp.mean(xf * xf, axis=-1, keepdims=True) + RMS_EPS)
    return (y * g.astype(jnp.float32)).astype(x.dtype)


def pool_mixer(u, w_grp, b_grp, scale):
    B, S, _ = u.shape
    ug = u.reshape(B, S, N_POOL_GROUPS, POOL_GROUP_DIM).astype(jnp.float32)
    c = jnp.pad(jnp.cumsum(ug, axis=1), ((0, 0), (1, 0), (0, 0), (0, 0)))
    t = jnp.arange(S)[:, None]
    w = jnp.asarray(POOL_WINDOWS, dtype=jnp.int32)[None, :]
    lo = jnp.maximum(t + 1 - w, 0)
    grp = jnp.arange(N_POOL_GROUPS)[None, :]
    window_sum = c[:, 1:] - c[:, lo, grp]
    count = jnp.minimum(t + 1, w).astype(jnp.float32)
    d = (window_sum / count[None, :, :, None] - ug).astype(u.dtype)
    y = jnp.einsum('bsgc,gcd->bsgd', d, w_grp) + b_grp
    return y.reshape(B, S, POOL_WIDTH) * scale


def swa_sink_attention(q, k, v, sinks):
    B, S, _ = q.shape
    nb = S // BLOCK
    qb = q.reshape(B, nb, BLOCK, N_KV_HEADS, GROUP, HEAD_DIM)
    kb = k.reshape(B, nb, BLOCK, N_KV_HEADS, HEAD_DIM)
    vb = v.reshape(B, nb, BLOCK, N_KV_HEADS, HEAD_DIM)
    pad = ((0, 0), (1, 0), (0, 0), (0, 0), (0, 0))
    kk = jnp.concatenate([jnp.pad(kb, pad)[:, :-1], kb], axis=2)
    vv = jnp.concatenate([jnp.pad(vb, pad)[:, :-1], vb], axis=2)
    s = jnp.einsum('bnqhgd,bnkhd->bnhgqk', qb, kk, preferred_element_type=jnp.float32)
    s = s * (1.0 / math.sqrt(HEAD_DIM))
    qi = jnp.arange(BLOCK)[:, None]
    kj = jnp.arange(2 * BLOCK)[None, :]
    dist = BLOCK + qi - kj
    kpos = jnp.arange(nb)[:, None] * BLOCK - BLOCK + kj
    valid = ((dist >= 0) & (dist < WINDOW))[None] & (kpos >= 0)[:, None, :]
    slopes = jnp.asarray(ALIBI_SLOPES).reshape(N_KV_HEADS, GROUP)
    s = s - slopes[None, None, :, :, None, None] * dist.astype(jnp.float32)[None, None, None, None]
    s = jnp.where(valid[None, :, None, None], s, NEG_INF)
    sink = sinks.astype(jnp.float32).reshape(N_KV_HEADS, GROUP)[None, None, :, :, None, None]
    m = jnp.maximum(jnp.max(s, axis=-1, keepdims=True), sink)
    p = jnp.exp(s - m)
    p = p / (jnp.sum(p, axis=-1, keepdims=True) + jnp.exp(sink - m))
    o = jnp.einsum('bnhgqk,bnkhd->bnqhgd', p.astype(v.dtype), vv)
    return o.reshape(B, S, ATTN_WIDTH)


def setup_inputs(seed: int = 0) -> dict:
    key = jax.random.key(seed)
    ks = jax.random.split(key, 16)
    f32 = jnp.float32
    nrm = lambda k, shape, fan_in: jax.random.normal(k, shape, f32) * (fan_in ** -0.5)
    return {
        "x": jax.random.normal(ks[0], (BATCH, SEQ, D_MODEL), f32),
        "norm_mix": 1.0 + 0.1 * jax.random.normal(ks[1], (DEPTH, D_MODEL), f32),
        "w_in": nrm(ks[2], (DEPTH, D_MODEL, IN_WIDTH), D_MODEL),
        "pool_w": nrm(ks[3], (DEPTH, N_POOL_GROUPS, POOL_GROUP_DIM, POOL_GROUP_DIM), POOL_GROUP_DIM),
        "pool_b": 0.02 * jax.random.normal(ks[4], (DEPTH, N_POOL_GROUPS, POOL_GROUP_DIM), f32),
        "pool_scale": 1.0 + 0.1 * jax.random.normal(ks[5], (DEPTH, POOL_WIDTH), f32),
        "attn_sinks": 0.5 * jax.random.normal(ks[6], (DEPTH, N_HEADS), f32),
        "p_pool": nrm(ks[7], (DEPTH, POOL_WIDTH, D_MODEL), POOL_WIDTH),
        "p_attn": nrm(ks[8], (DEPTH, ATTN_WIDTH, D_MODEL), ATTN_WIDTH),
        "w_out": nrm(ks[9], (DEPTH, D_MODEL, D_MODEL), D_MODEL),
        "norm_mlp": 1.0 + 0.1 * jax.random.normal(ks[10], (DEPTH, D_MODEL), f32),
        "w_up": nrm(ks[11], (DEPTH, D_MODEL, D_FF), D_MODEL),
        "w_down": nrm(ks[12], (DEPTH, D_FF, D_MODEL), D_FF),
        "norm_final": 1.0 + 0.1 * jax.random.normal(ks[13], (D_MODEL,), f32),
    }


def reference(x, norm_mix, w_in, pool_w, pool_b, pool_scale, attn_sinks, p_pool, p_attn,
              w_out, norm_mlp, w_up, w_down, norm_final):
    h = x
    c1 = POOL_WIDTH
    c2 = c1 + ATTN_WIDTH
    c3 = c2 + KV_WIDTH
    c4 = c3 + KV_WIDTH
    c5 = c4 + D_MODEL
    for l in range(DEPTH):
        u = rms_norm(h, norm_mix[l])
        z = u @ w_in[l]
        u_pool = z[..., :c1]
        q = z[..., c1:c2]
        k = z[..., c2:c3]
        v = z[..., c3:c4]
        gate_pool = jax.nn.sigmoid(z[..., c4:c5])
        gate_attn = jax.nn.sigmoid(z[..., c5:])
        y_pool = pool_mixer(u_pool, pool_w[l], pool_b[l], pool_scale[l]) @ p_pool[l]
        y_attn = swa_sink_attention(q, k, v, attn_sinks[l]) @ p_attn[l]
        mixed = gate_pool * y_pool + gate_attn * y_attn
        h = h + mixed @ w_out[l]
        u2 = rms_norm(h, norm_mlp[l])
        a = jax.nn.relu(u2 @ w_up[l])
        h = h + (a * a) @ w_down[l]
    return rms_norm(h, norm_final)
```

```python
import functools

import jax
import jax.numpy as jnp
import numpy as np
from jax import lax
from jax.experimental import pallas as pl
from jax.experimental.pallas import tpu as pltpu

HEAD_DIM = 64
N_KV_HEADS = 2
GROUP = 4
BLOCK = 128
POOL_WINDOWS = (2, 4, 8, 16)
POOL_GROUP_DIM = 128
POOL_HALO = 32
RMS_EPS = 1e-5
NEG_INF = -1e30
LANES = 128

TOKEN_TILE = 512

_F32 = jnp.float32
_BF16 = jnp.bfloat16


def _dot(a, b):
    return jnp.dot(a, b, preferred_element_type=_F32)


def _rms_norm(x, g):
    return x * lax.rsqrt(jnp.mean(x * x, axis=-1, keepdims=True) + RMS_EPS) * g


def _attention_bias():
    qi = np.arange(BLOCK)[:, None]
    kj = np.arange(2 * BLOCK)[None, :]
    dist = (BLOCK + qi - kj).astype(np.float32)
    valid = (dist >= 0) & (dist < BLOCK)
    n_heads = N_KV_HEADS * GROUP
    out = np.empty((N_KV_HEADS, GROUP * BLOCK, 2 * BLOCK), np.float32)
    for h in range(N_KV_HEADS):
        for g in range(GROUP):
            slope = np.float32(2.0 ** (-8.0 * (h * GROUP + g + 1) / n_heads))
            out[h, g * BLOCK:(g + 1) * BLOCK] = np.where(valid, -(slope * dist), np.float32(NEG_INF))
    return out


def _mixer_kernel(sinks_ref, x_ref, g_ref, w_in_ref, pool_w_ref, pool_b_ref, pool_scale_ref,
                  bias_ref, p_pool_ref, p_attn_ref, w_out_ref, o_ref,
                  pool_hist, pool_stage, k_ext, v_ext, *, tm):
    i = pl.program_id(1)
    c_pool = len(POOL_WINDOWS) * POOL_GROUP_DIM
    c_q = c_pool + N_KV_HEADS * GROUP * HEAD_DIM
    c_k = c_q + N_KV_HEADS * HEAD_DIM
    c_v = c_k + N_KV_HEADS * HEAD_DIM
    d_model = x_ref.shape[-1]
    c_gp = c_v + d_model

    @pl.when(i == 0)
    def _():
        pool_hist[0:POOL_HALO, :] = jnp.zeros((POOL_HALO, c_pool), _F32)
        k_ext[:, 0:BLOCK, :] = jnp.zeros((N_KV_HEADS, BLOCK, LANES), _BF16)
        v_ext[:, 0:BLOCK, :] = jnp.zeros((N_KV_HEADS, BLOCK, LANES), _BF16)

    x = x_ref[...]
    ub = _rms_norm(x, g_ref[...]).astype(_BF16)

    up = _dot(ub, w_in_ref[:, 0:c_pool])
    rows = tm + POOL_HALO
    pool_hist[POOL_HALO:rows, :] = up
    pool_stage[0, 8:rows, :] = pool_hist[8:rows, :] + pool_hist[7:rows - 1, :]
    pool_stage[1, 16:rows, 128:] = pool_stage[0, 16:rows, 128:] + pool_stage[0, 14:rows - 2, 128:]
    pool_stage[2, 24:rows, 256:] = pool_stage[1, 24:rows, 256:] + pool_stage[1, 20:rows - 4, 256:]
    s16 = pool_stage[2, 32:rows, 384:] + pool_stage[2, 24:rows - 8, 384:]
    wsum = jnp.concatenate([pool_stage[0, 32:rows, 0:128], pool_stage[1, 32:rows, 128:256],
                            pool_stage[2, 32:rows, 256:384], s16], axis=1)
    t1 = i * tm + lax.broadcasted_iota(jnp.int32, (tm, c_pool), 0) + 1
    win = jnp.left_shift(2, lax.broadcasted_iota(jnp.int32, (tm, c_pool), 1) // POOL_GROUP_DIM)
    count = jnp.minimum(t1, win).astype(_F32)
    d = (wsum / count - up).astype(_BF16)
    y = jnp.concatenate(
        [_dot(d[:, g * POOL_GROUP_DIM:(g + 1) * POOL_GROUP_DIM], pool_w_ref[g])
         for g in range(len(POOL_WINDOWS))], axis=1)
    y = (y + pool_b_ref[...]) * pool_scale_ref[...]
    y_pool = _dot(y.astype(_BF16), p_pool_ref[...])
    mixed = jax.nn.sigmoid(_dot(ub, w_in_ref[:, c_v:c_gp])) * y_pool
    pool_hist[0:POOL_HALO, :] = pool_hist[tm:rows, :]

    q = _dot(ub, w_in_ref[:, c_pool:c_q]) * (1.0 / float(np.sqrt(HEAD_DIM)))
    kv = _dot(ub, w_in_ref[:, c_q:c_v])
    lane = lax.broadcasted_iota(jnp.int32, (tm, LANES), 1)
    low = lane < HEAD_DIM
    for t, ext in ((kv[:, 0:LANES], k_ext), (kv[:, LANES:2 * LANES], v_ext)):
        swapped = pltpu.roll(t, HEAD_DIM, axis=1)
        ext[0, BLOCK:BLOCK + tm, :] = jnp.where(low, t, swapped).astype(_BF16)
        ext[1, BLOCK:BLOCK + tm, :] = jnp.where(low, swapped, t).astype(_BF16)

    low_b = lax.broadcasted_iota(jnp.int32, (BLOCK, LANES), 1) < HEAD_DIM
    kcol = lax.broadcasted_iota(jnp.int32, (GROUP * BLOCK, 2 * BLOCK), 1)
    pad_keys = jnp.logical_and(kcol < BLOCK, i == 0)
    zeros_b = jnp.zeros((BLOCK, LANES), _F32)
    attn_cols = [[None] * (tm // BLOCK) for _ in range(N_KV_HEADS * GROUP // 2)]
    for h in range(N_KV_HEADS):
        sink = jnp.concatenate(
            [jnp.full((BLOCK, 1), sinks_ref[h * GROUP + g], _F32) for g in range(GROUP)], axis=0)
        bias = bias_ref[h]
        for j in range(tm // BLOCK):
            r0 = j * BLOCK
            qa = q[r0:r0 + BLOCK, (2 * h) * LANES:(2 * h + 1) * LANES]
            qb = q[r0:r0 + BLOCK, (2 * h + 1) * LANES:(2 * h + 2) * LANES]
            lhs = jnp.concatenate([jnp.where(low_b, qa, zeros_b), jnp.where(low_b, zeros_b, qa),
                                   jnp.where(low_b, qb, zeros_b), jnp.where(low_b, zeros_b, qb)],
                                  axis=0).astype(_BF16)
            keys = k_ext[h, r0:r0 + 2 * BLOCK, :]
            vals = v_ext[h, r0:r0 + 2 * BLOCK, :]
            s = lax.dot_general(lhs, keys, (((1,), (1,)), ((), ())), preferred_element_type=_F32)
            s = s + bias
            if j == 0:
                s = jnp.where(pad_keys, NEG_INF, s)
            m = jnp.maximum(jnp.max(s, axis=-1, keepdims=True), sink)
            p = jnp.exp(s - m)
            denom = jnp.sum(p, axis=-1, keepdims=True) + jnp.exp(sink - m)
            o = _dot(p.astype(_BF16), vals) * (1.0 / denom)
            attn_cols[2 * h][j] = jnp.where(low_b, o[0:BLOCK], o[BLOCK:2 * BLOCK])
            attn_cols[2 * h + 1][j] = jnp.where(low_b, o[2 * BLOCK:3 * BLOCK], o[3 * BLOCK:4 * BLOCK])
    attn = jnp.concatenate([jnp.concatenate(col, axis=0) for col in attn_cols], axis=1)
    k_ext[:, 0:BLOCK, :] = k_ext[:, tm:tm + BLOCK, :]
    v_ext[:, 0:BLOCK, :] = v_ext[:, tm:tm + BLOCK, :]

    y_attn = _dot(attn.astype(_BF16), p_attn_ref[...])
    mixed = mixed + jax.nn.sigmoid(_dot(ub, w_in_ref[:, c_gp:c_gp + d_model])) * y_attn
    o_ref[...] = x + _dot(mixed.astype(_BF16), w_out_ref[...])


def _mlp_kernel(h_ref, g_ref, w_up_ref, w_down_ref, gf_ref, o_ref, *, ff_chunk, final_norm):
    h = h_ref[...]
    ub = _rms_norm(h, g_ref[...]).astype(_BF16)
    d_ff = w_up_ref.shape[1]
    acc = h
    for c in range(0, d_ff, ff_chunk):
        a = jnp.maximum(_dot(ub, w_up_ref[:, c:c + ff_chunk]), 0.0)
        acc = acc + _dot((a * a).astype(_BF16), w_down_ref[c:c + ff_chunk, :])
    if final_norm:
        acc = _rms_norm(acc, gf_ref[...])
    o_ref[...] = acc


def _resident(shape):
    zeros = (0,) * len(shape)
    return pl.BlockSpec(shape, lambda *_: zeros, pipeline_mode=pl.Buffered(1))


def _mixer(x, sinks, g, w_in, pool_w, pool_b, pool_scale, bias, p_pool, p_attn, w_out):
    b, s, d = x.shape
    tm = TOKEN_TILE
    assert s % tm == 0 and tm % BLOCK == 0
    c_pool = len(POOL_WINDOWS) * POOL_GROUP_DIM
    grid_spec = pltpu.PrefetchScalarGridSpec(
        num_scalar_prefetch=1,
        grid=(b, s // tm),
        in_specs=[
            pl.BlockSpec((None, tm, d), lambda bi, i, _: (bi, i, 0)),
            _resident(g.shape), _resident(w_in.shape), _resident(pool_w.shape),
            _resident(pool_b.shape), _resident(pool_scale.shape), _resident(bias.shape),
            _resident(p_pool.shape), _resident(p_attn.shape), _resident(w_out.shape),
        ],
        out_specs=pl.BlockSpec((None, tm, d), lambda bi, i, _: (bi, i, 0)),
        scratch_shapes=[
            pltpu.VMEM((tm + POOL_HALO, c_pool), _F32),
            pltpu.VMEM((3, tm + POOL_HALO, c_pool), _F32),
            pltpu.VMEM((N_KV_HEADS, tm + BLOCK, LANES), _BF16),
            pltpu.VMEM((N_KV_HEADS, tm + BLOCK, LANES), _BF16),
        ],
    )
    return pl.pallas_call(
        functools.partial(_mixer_kernel, tm=tm),
        grid_spec=grid_spec,
        out_shape=jax.ShapeDtypeStruct(x.shape, _F32),
        compiler_params=pltpu.CompilerParams(dimension_semantics=("arbitrary", "arbitrary")),
        name="mixer",
    )(sinks, x, g, w_in, pool_w, pool_b, pool_scale, bias, p_pool, p_attn, w_out)


def _mlp(h, g, w_up, w_down, g_final, final_norm):
    t, d = h.shape
    tm = TOKEN_TILE
    assert t % tm == 0
    return pl.pallas_call(
        functools.partial(_mlp_kernel, ff_chunk=1024, final_norm=final_norm),
        grid=(t // tm,),
        in_specs=[
            pl.BlockSpec((tm, d), lambda i: (i, 0)),
            _resident(g.shape), _resident(w_up.shape), _resident(w_down.shape), _resident(g_final.shape),
        ],
        out_specs=pl.BlockSpec((tm, d), lambda i: (i, 0)),
        out_shape=jax.ShapeDtypeStruct(h.shape, _F32),
        compiler_params=pltpu.CompilerParams(dimension_semantics=("arbitrary",)),
        name="mlp",
    )(h, g, w_up, w_down, g_final)


def kernel(x, norm_mix, w_in, pool_w, pool_b, pool_scale, attn_sinks, p_pool, p_attn,
           w_out, norm_mlp, w_up, w_down, norm_final):
    depth = norm_mix.shape[0]
    b, s, d = x.shape
    bias = jnp.asarray(_attention_bias())
    h = x
    for l in range(depth):
        h = _mixer(h, attn_sinks[l], norm_mix[l][None, :], w_in[l].astype(_BF16),
                   pool_w[l].astype(_BF16), pool_b[l].reshape(1, -1), pool_scale[l][None, :], bias,
                   p_pool[l].astype(_BF16), p_attn[l].astype(_BF16), w_out[l].astype(_BF16))
        h = _mlp(h.reshape(b * s, d), norm_mlp[l][None, :], w_up[l].astype(_BF16),
                 w_down[l].astype(_BF16), norm_final[None, :], l == depth - 1).reshape(b, s, d)
    return h
```

```python
import functools

import jax
import jax.numpy as jnp
import numpy as np
from jax import lax
from jax.experimental import pallas as pl
from jax.experimental.pallas import tpu as pltpu

HEAD_DIM = 64
N_KV_HEADS = 2
GROUP = 4
BLOCK = 128
POOL_WINDOWS = (2, 4, 8, 16)
POOL_GROUP_DIM = 128
POOL_HALO = 32
RMS_EPS = 1e-5
NEG_INF = -1e30
LANES = 128

TOKEN_TILE = 512

_F32 = jnp.float32
_BF16 = jnp.bfloat16


def _rms_norm(x, g):
    return x * lax.rsqrt(jnp.mean(x * x, axis=-1, keepdims=True) + RMS_EPS) * g


def _attention_bias():
    kj = np.arange(2 * BLOCK)[:, None]
    qi = np.arange(BLOCK)[None, :]
    dist = (BLOCK + qi - kj).astype(np.float32)
    valid = (dist >= 0) & (dist < BLOCK)
    n_heads = N_KV_HEADS * GROUP
    out = np.empty((N_KV_HEADS, 2 * BLOCK, GROUP * BLOCK), np.float32)
    for h in range(N_KV_HEADS):
        for g in range(GROUP):
            slope = np.float32(2.0 ** (-8.0 * (h * GROUP + g + 1) / n_heads))
            out[h, :, g * BLOCK:(g + 1) * BLOCK] = np.where(valid, -(slope * dist), np.float32(NEG_INF))
    return out


def _mixer_kernel(sinks_ref, x_ref, g_ref, w_in_ref, pool_w_ref, pool_b_ref, pool_scale_ref,
                  bias_ref, p_pool_ref, p_attn_ref, w_out_ref, o_ref,
                  pool_hist, pool_stage, pool_bd, k_ext, vt_ext, *, tm):
    i = pl.program_id(1)
    n_groups = len(POOL_WINDOWS)
    c_pool = n_groups * POOL_GROUP_DIM
    c_q = c_pool + N_KV_HEADS * GROUP * HEAD_DIM
    c_v = c_q + 2 * N_KV_HEADS * HEAD_DIM
    d_model = x_ref.shape[-1]
    c_gp = c_v + d_model
    n_blocks = tm // BLOCK

    @pl.when(i == 0)
    def _():
        pool_hist[0:POOL_HALO, :] = jnp.zeros((POOL_HALO, c_pool), _F32)
        k_ext[:, 0:BLOCK, :] = jnp.zeros((N_KV_HEADS, BLOCK, LANES), _BF16)
        vt_ext[:, :, 0:BLOCK] = jnp.zeros((N_KV_HEADS, LANES, BLOCK), _BF16)
        pool_bd[...] = jnp.zeros((c_pool, c_pool), _BF16)
        for g in range(n_groups):
            lo = g * POOL_GROUP_DIM
            pool_bd[lo:lo + POOL_GROUP_DIM, lo:lo + POOL_GROUP_DIM] = pool_w_ref[g]

    x = x_ref[...]
    ub = _rms_norm(x, g_ref[...]).astype(_BF16)

    up = jnp.dot(ub, w_in_ref[:, 0:c_pool], preferred_element_type=_F32)
    kv = jnp.dot(ub, w_in_ref[:, c_q:c_v], preferred_element_type=_F32)
    q = jnp.dot(ub, w_in_ref[:, c_pool:c_q], preferred_element_type=_F32) * (1.0 / float(np.sqrt(HEAD_DIM)))
    gate_pool = jax.nn.sigmoid(jnp.dot(ub, w_in_ref[:, c_v:c_gp], preferred_element_type=_F32))

    rows = tm + POOL_HALO
    pool_hist[POOL_HALO:rows, :] = up
    pool_stage[0, 8:rows, :] = pool_hist[8:rows, :] + pool_hist[7:rows - 1, :]
    pool_stage[1, 16:rows, 128:] = pool_stage[0, 16:rows, 128:] + pool_stage[0, 14:rows - 2, 128:]
    pool_stage[2, 24:rows, 256:] = pool_stage[1, 24:rows, 256:] + pool_stage[1, 20:rows - 4, 256:]
    s16 = pool_stage[2, 32:rows, 384:] + pool_stage[2, 24:rows - 8, 384:]
    wsum = jnp.concatenate([pool_stage[0, 32:rows, 0:128], pool_stage[1, 32:rows, 128:256],
                            pool_stage[2, 32:rows, 256:384], s16], axis=1)
    t1 = i * tm + lax.broadcasted_iota(jnp.int32, (tm, c_pool), 0) + 1
    win = jnp.left_shift(2, lax.broadcasted_iota(jnp.int32, (tm, c_pool), 1) // POOL_GROUP_DIM)
    count = jnp.minimum(t1, win).astype(_F32)
    d = (wsum / count - up).astype(_BF16)
    y = jnp.dot(d, pool_bd[...], preferred_element_type=_F32)
    y = ((y + pool_b_ref[...]) * pool_scale_ref[...]).astype(_BF16)
    pool_hist[0:POOL_HALO, :] = pool_hist[tm:rows, :]

    low = lax.broadcasted_iota(jnp.int32, (tm, LANES), 1) < HEAD_DIM
    k = kv[:, 0:LANES]
    k_swapped = pltpu.roll(k, HEAD_DIM, axis=1)
    k_ext[0, BLOCK:BLOCK + tm, :] = jnp.where(low, k, k_swapped).astype(_BF16)
    k_ext[1, BLOCK:BLOCK + tm, :] = jnp.where(low, k_swapped, k).astype(_BF16)
    vt = kv[:, LANES:2 * LANES].T.astype(_BF16)
    for h in range(N_KV_HEADS):
        vh = vt[h * HEAD_DIM:(h + 1) * HEAD_DIM]
        vt_ext[h, :, BLOCK:BLOCK + tm] = jnp.concatenate([vh, vh], axis=0)

    low_b = lax.broadcasted_iota(jnp.int32, (BLOCK, LANES), 1) < HEAD_DIM
    krow = lax.broadcasted_iota(jnp.int32, (2 * BLOCK, GROUP * BLOCK), 0)
    pad_keys = jnp.logical_and(krow < BLOCK, i == 0)
    zeros_b = jnp.zeros((BLOCK, LANES), _F32)
    sinks = [jnp.concatenate([jnp.full((1, BLOCK), sinks_ref[h * GROUP + g], _F32) for g in range(GROUP)], axis=1)
             for h in range(N_KV_HEADS)]

    def scores(h, j):
        r0 = j * BLOCK
        qa = q[r0:r0 + BLOCK, (2 * h) * LANES:(2 * h + 1) * LANES]
        qb = q[r0:r0 + BLOCK, (2 * h + 1) * LANES:(2 * h + 2) * LANES]
        lhs = jnp.concatenate([jnp.where(low_b, qa, zeros_b), jnp.where(low_b, zeros_b, qa),
                               jnp.where(low_b, qb, zeros_b), jnp.where(low_b, zeros_b, qb)],
                              axis=0).astype(_BF16)
        keys = k_ext[h, r0:r0 + 2 * BLOCK, :]
        return lax.dot_general(keys, lhs, (((1,), (1,)), ((), ())), preferred_element_type=_F32)

    def probs(h, j, s):
        s = s + bias_ref[h]
        if j == 0:
            s = jnp.where(pad_keys, NEG_INF, s)
        m = jnp.maximum(jnp.max(s, axis=0, keepdims=True), sinks[h])
        p = jnp.exp(s - m)
        denom = jnp.sum(p, axis=0, keepdims=True) + jnp.exp(sinks[h] - m)
        return p.astype(_BF16), 1.0 / denom

    def weighted_values(h, j, p, inv):
        r0 = j * BLOCK
        ot = jnp.dot(vt_ext[h, :, r0:r0 + 2 * BLOCK], p, preferred_element_type=_F32) * inv
        o = ot.T
        return (jnp.where(low_b, o[0:BLOCK], o[BLOCK:2 * BLOCK]),
                jnp.where(low_b, o[2 * BLOCK:3 * BLOCK], o[3 * BLOCK:4 * BLOCK]))

    steps = [(h, j) for h in range(N_KV_HEADS) for j in range(n_blocks)]
    lookahead = 2
    gate_chunk = 2 * LANES
    gate_every = max(1, len(steps) * gate_chunk // d_model)
    pending = [scores(*st) for st in steps[:lookahead]]
    y_pool = jnp.dot(y, p_pool_ref[...], preferred_element_type=_F32)
    mixed = gate_pool * y_pool
    attn_cols = [[None] * n_blocks for _ in range(N_KV_HEADS * GROUP // 2)]
    gate_attn = []
    for n, (h, j) in enumerate(steps):
        if n % gate_every == 0 and len(gate_attn) * gate_chunk < d_model:
            c0 = c_gp + len(gate_attn) * gate_chunk
            gate_attn.append(jax.nn.sigmoid(
                jnp.dot(ub, w_in_ref[:, c0:c0 + gate_chunk], preferred_element_type=_F32)))
        p, inv = probs(h, j, pending.pop(0))
        if n + lookahead < len(steps):
            pending.append(scores(*steps[n + lookahead]))
        attn_cols[2 * h][j], attn_cols[2 * h + 1][j] = weighted_values(h, j, p, inv)
    attn = jnp.concatenate([jnp.concatenate(col, axis=0) for col in attn_cols], axis=1)
    k_ext[:, 0:BLOCK, :] = k_ext[:, tm:tm + BLOCK, :]
    vt_ext[:, :, 0:BLOCK] = vt_ext[:, :, tm:tm + BLOCK]

    y_attn = jnp.dot(attn.astype(_BF16), p_attn_ref[...], preferred_element_type=_F32)
    mixed = mixed + jnp.concatenate(gate_attn, axis=1) * y_attn
    o_ref[...] = x + jnp.dot(mixed.astype(_BF16), w_out_ref[...], preferred_element_type=_F32)


def _mlp_kernel(h_ref, g_ref, w_up_ref, w_down_ref, gf_ref, o_ref, *, ff_chunk, final_norm):
    h = h_ref[...]
    ub = _rms_norm(h, g_ref[...]).astype(_BF16)
    d_ff = w_up_ref.shape[1]
    acc = h
    for c in range(0, d_ff, ff_chunk):
        a = jnp.maximum(jnp.dot(ub, w_up_ref[:, c:c + ff_chunk], preferred_element_type=_F32), 0.0)
        acc = acc + jnp.dot((a * a).astype(_BF16), w_down_ref[c:c + ff_chunk, :], preferred_element_type=_F32)
    if final_norm:
        acc = _rms_norm(acc, gf_ref[...])
    o_ref[...] = acc


def _resident(shape):
    zeros = (0,) * len(shape)
    return pl.BlockSpec(shape, lambda *_: zeros, pipeline_mode=pl.Buffered(1))


def _mixer(x, sinks, g, w_in, pool_w, pool_b, pool_scale, bias, p_pool, p_attn, w_out):
    b, s, d = x.shape
    tm = TOKEN_TILE
    assert s % tm == 0 and tm % BLOCK == 0
    c_pool = len(POOL_WINDOWS) * POOL_GROUP_DIM
    grid_spec = pltpu.PrefetchScalarGridSpec(
        num_scalar_prefetch=1,
        grid=(b, s // tm),
        in_specs=[
            pl.BlockSpec((None, tm, d), lambda bi, i, _: (bi, i, 0)),
            _resident(g.shape), _resident(w_in.shape), _resident(pool_w.shape),
            _resident(pool_b.shape), _resident(pool_scale.shape), _resident(bias.shape),
            _resident(p_pool.shape), _resident(p_attn.shape), _resident(w_out.shape),
        ],
        out_specs=pl.BlockSpec((None, tm, d), lambda bi, i, _: (bi, i, 0)),
        scratch_shapes=[
            pltpu.VMEM((tm + POOL_HALO, c_pool), _F32),
            pltpu.VMEM((3, tm + POOL_HALO, c_pool), _F32),
            pltpu.VMEM((c_pool, c_pool), _BF16),
            pltpu.VMEM((N_KV_HEADS, tm + BLOCK, LANES), _BF16),
            pltpu.VMEM((N_KV_HEADS, LANES, tm + BLOCK), _BF16),
        ],
    )
    return pl.pallas_call(
        functools.partial(_mixer_kernel, tm=tm),
        grid_spec=grid_spec,
        out_shape=jax.ShapeDtypeStruct(x.shape, _F32),
        compiler_params=pltpu.CompilerParams(dimension_semantics=("arbitrary", "arbitrary")),
        name="mixer",
    )(sinks, x, g, w_in, pool_w, pool_b, pool_scale, bias, p_pool, p_attn, w_out)


def _mlp(h, g, w_up, w_down, g_final, final_norm):
    t, d = h.shape
    tm = TOKEN_TILE
    assert t % tm == 0
    return pl.pallas_call(
        functools.partial(_mlp_kernel, ff_chunk=1024, final_norm=final_norm),
        grid=(t // tm,),
        in_specs=[
            pl.BlockSpec((tm, d), lambda i: (i, 0)),
            _resident(g.shape), _resident(w_up.shape), _resident(w_down.shape), _resident(g_final.shape),
        ],
        out_specs=pl.BlockSpec((tm, d), lambda i: (i, 0)),
        out_shape=jax.ShapeDtypeStruct(h.shape, _F32),
        compiler_params=pltpu.CompilerParams(dimension_semantics=("arbitrary",)),
        name="mlp",
    )(h, g, w_up, w_down, g_final)


def kernel(x, norm_mix, w_in, pool_w, pool_b, pool_scale, attn_sinks, p_pool, p_attn,
           w_out, norm_mlp, w_up, w_down, norm_final):
    depth = norm_mix.shape[0]
    b, s, d = x.shape
    bias = jnp.asarray(_attention_bias())
    h = x
    for l in range(depth):
        h = _mixer(h, attn_sinks[l], norm_mix[l][None, :], w_in[l].astype(_BF16),
                   pool_w[l].astype(_BF16), pool_b[l].reshape(1, -1), pool_scale[l][None, :], bias,
                   p_pool[l].astype(_BF16), p_attn[l].astype(_BF16), w_out[l].astype(_BF16))
        h = _mlp(h.reshape(b * s, d), norm_mlp[l][None, :], w_up[l].astype(_BF16),
                 w_down[l].astype(_BF16), norm_final[None, :], l == depth - 1).reshape(b, s, d)
    return h
```

```python
import functools

import jax
import jax.numpy as jnp
import numpy as np
from jax import lax
from jax.experimental import pallas as pl
from jax.experimental.pallas import tpu as pltpu

HEAD_DIM = 64
N_KV_HEADS = 2
GROUP = 4
BLOCK = 128
POOL_WINDOWS = (2, 4, 8, 16)
POOL_GROUP_DIM = 128
POOL_HALO = 32
RMS_EPS = 1e-5
NEG_INF = -1e30
LANES = 128

MIXER_TILE = 1024
MLP_TILE = 1024

_F32 = jnp.float32
_BF16 = jnp.bfloat16


def _rms_norm(x, g):
    return x * lax.rsqrt(jnp.mean(x * x, axis=-1, keepdims=True) + RMS_EPS) * g


def _attention_bias():
    kj = np.arange(2 * BLOCK)[:, None]
    qi = np.arange(BLOCK)[None, :]
    dist = (BLOCK + qi - kj).astype(np.float32)
    valid = (dist >= 0) & (dist < BLOCK)
    n_heads = N_KV_HEADS * GROUP
    out = np.empty((N_KV_HEADS, 2 * BLOCK, GROUP * BLOCK), np.float32)
    for h in range(N_KV_HEADS):
        for g in range(GROUP):
            slope = np.float32(2.0 ** (-8.0 * (h * GROUP + g + 1) / n_heads))
            out[h, :, g * BLOCK:(g + 1) * BLOCK] = np.where(valid, -(slope * dist), np.float32(NEG_INF))
    return out


def _mixer_kernel(sinks_ref, x_ref, g_ref, w_in_ref, pool_w_ref, pool_b_ref, pool_scale_ref,
                  bias_ref, p_pool_ref, p_attn_ref, w_out_ref, o_ref,
                  pool_hist, pool_stage, pool_bd, k_ext, vt_ext, *, tm):
    i = pl.program_id(1)
    n_groups = len(POOL_WINDOWS)
    c_pool = n_groups * POOL_GROUP_DIM
    c_q = c_pool + N_KV_HEADS * GROUP * HEAD_DIM
    c_v = c_q + 2 * N_KV_HEADS * HEAD_DIM
    d_model = x_ref.shape[-1]
    c_gp = c_v + d_model
    n_blocks = tm // BLOCK

    @pl.when(i == 0)
    def _():
        pool_hist[0:POOL_HALO, :] = jnp.zeros((POOL_HALO, c_pool), _F32)
        k_ext[:, 0:BLOCK, :] = jnp.zeros((N_KV_HEADS, BLOCK, LANES), _BF16)
        vt_ext[:, :, 0:BLOCK] = jnp.zeros((N_KV_HEADS, LANES, BLOCK), _BF16)
        pool_bd[...] = jnp.zeros((c_pool, c_pool), _BF16)
        for g in range(n_groups):
            lo = g * POOL_GROUP_DIM
            pool_bd[lo:lo + POOL_GROUP_DIM, lo:lo + POOL_GROUP_DIM] = pool_w_ref[g]

    x = x_ref[...]
    ub = _rms_norm(x, g_ref[...]).astype(_BF16)

    up = jnp.dot(ub, w_in_ref[:, 0:c_pool], preferred_element_type=_F32)
    kv = jnp.dot(ub, w_in_ref[:, c_q:c_v], preferred_element_type=_F32)
    q = jnp.dot(ub, w_in_ref[:, c_pool:c_q], preferred_element_type=_F32) * (1.0 / float(np.sqrt(HEAD_DIM)))
    gate_pool = jax.nn.sigmoid(jnp.dot(ub, w_in_ref[:, c_v:c_gp], preferred_element_type=_F32))

    rows = tm + POOL_HALO
    pool_hist[POOL_HALO:rows, :] = up
    pool_stage[0, 8:rows, :] = pool_hist[8:rows, :] + pool_hist[7:rows - 1, :]
    pool_stage[1, 16:rows, 128:] = pool_stage[0, 16:rows, 128:] + pool_stage[0, 14:rows - 2, 128:]
    pool_stage[2, 24:rows, 256:] = pool_stage[1, 24:rows, 256:] + pool_stage[1, 20:rows - 4, 256:]
    s16 = pool_stage[2, 32:rows, 384:] + pool_stage[2, 24:rows - 8, 384:]
    wsum = jnp.concatenate([pool_stage[0, 32:rows, 0:128], pool_stage[1, 32:rows, 128:256],
                            pool_stage[2, 32:rows, 256:384], s16], axis=1)
    t1 = i * tm + lax.broadcasted_iota(jnp.int32, (tm, c_pool), 0) + 1
    win = jnp.left_shift(2, lax.broadcasted_iota(jnp.int32, (tm, c_pool), 1) // POOL_GROUP_DIM)
    count = jnp.minimum(t1, win).astype(_F32)
    d = (wsum / count - up).astype(_BF16)
    y = jnp.dot(d, pool_bd[...], preferred_element_type=_F32)
    y = ((y + pool_b_ref[...]) * pool_scale_ref[...]).astype(_BF16)
    pool_hist[0:POOL_HALO, :] = pool_hist[tm:rows, :]

    low = lax.broadcasted_iota(jnp.int32, (tm, LANES), 1) < HEAD_DIM
    k = kv[:, 0:LANES]
    k_swapped = pltpu.roll(k, HEAD_DIM, axis=1)
    k_ext[0, BLOCK:BLOCK + tm, :] = jnp.where(low, k, k_swapped).astype(_BF16)
    k_ext[1, BLOCK:BLOCK + tm, :] = jnp.where(low, k_swapped, k).astype(_BF16)
    vt = kv[:, LANES:2 * LANES].T.astype(_BF16)
    for h in range(N_KV_HEADS):
        vh = vt[h * HEAD_DIM:(h + 1) * HEAD_DIM]
        vt_ext[h, :, BLOCK:BLOCK + tm] = jnp.concatenate([vh, vh], axis=0)

    low_b = lax.broadcasted_iota(jnp.int32, (BLOCK, LANES), 1) < HEAD_DIM
    krow = lax.broadcasted_iota(jnp.int32, (2 * BLOCK, GROUP * BLOCK), 0)
    pad_keys = jnp.logical_and(krow < BLOCK, i == 0)
    zeros_b = jnp.zeros((BLOCK, LANES), _F32)
    sinks = [jnp.concatenate([jnp.full((1, BLOCK), sinks_ref[h * GROUP + g], _F32) for g in range(GROUP)], axis=1)
             for h in range(N_KV_HEADS)]

    def scores(h, j):
        r0 = j * BLOCK
        qa = q[r0:r0 + BLOCK, (2 * h) * LANES:(2 * h + 1) * LANES]
        qb = q[r0:r0 + BLOCK, (2 * h + 1) * LANES:(2 * h + 2) * LANES]
        lhs = jnp.concatenate([jnp.where(low_b, qa, zeros_b), jnp.where(low_b, zeros_b, qa),
                               jnp.where(low_b, qb, zeros_b), jnp.where(low_b, zeros_b, qb)],
                              axis=0).astype(_BF16)
        keys = k_ext[h, r0:r0 + 2 * BLOCK, :]
        return lax.dot_general(keys, lhs, (((1,), (1,)), ((), ())), preferred_element_type=_F32)

    def probs(h, j, s):
        s = s + bias_ref[h]
        if j == 0:
            s = jnp.where(pad_keys, NEG_INF, s)
        m = jnp.maximum(jnp.max(s, axis=0, keepdims=True), sinks[h])
        p = jnp.exp(s - m)
        denom = jnp.sum(p, axis=0, keepdims=True) + jnp.exp(sinks[h] - m)
        return p.astype(_BF16), 1.0 / denom

    def weighted_values(h, j, p, inv):
        r0 = j * BLOCK
        ot = jnp.dot(vt_ext[h, :, r0:r0 + 2 * BLOCK], p, preferred_element_type=_F32) * inv
        o = ot.T
        return (jnp.where(low_b, o[0:BLOCK], o[BLOCK:2 * BLOCK]),
                jnp.where(low_b, o[2 * BLOCK:3 * BLOCK], o[3 * BLOCK:4 * BLOCK]))

    steps = [(h, j) for h in range(N_KV_HEADS) for j in range(n_blocks)]
    lookahead = 2
    gate_chunk = 2 * LANES
    gate_every = max(1, len(steps) * gate_chunk // d_model)
    pending = [scores(*st) for st in steps[:lookahead]]
    y_pool = jnp.dot(y, p_pool_ref[...], preferred_element_type=_F32)
    mixed = gate_pool * y_pool
    attn_cols = [[None] * n_blocks for _ in range(N_KV_HEADS * GROUP // 2)]
    gate_attn = []
    for n, (h, j) in enumerate(steps):
        if n % gate_every == 0 and len(gate_attn) * gate_chunk < d_model:
            c0 = c_gp + len(gate_attn) * gate_chunk
            gate_attn.append(jax.nn.sigmoid(
                jnp.dot(ub, w_in_ref[:, c0:c0 + gate_chunk], preferred_element_type=_F32)))
        p, inv = probs(h, j, pending.pop(0))
        if n + lookahead < len(steps):
            pending.append(scores(*steps[n + lookahead]))
        attn_cols[2 * h][j], attn_cols[2 * h + 1][j] = weighted_values(h, j, p, inv)
    attn = jnp.concatenate([jnp.concatenate(col, axis=0) for col in attn_cols], axis=1)
    k_ext[:, 0:BLOCK, :] = k_ext[:, tm:tm + BLOCK, :]
    vt_ext[:, :, 0:BLOCK] = vt_ext[:, :, tm:tm + BLOCK]

    y_attn = jnp.dot(attn.astype(_BF16), p_attn_ref[...], preferred_element_type=_F32)
    mixed = mixed + jnp.concatenate(gate_attn, axis=1) * y_attn
    o_ref[...] = x + jnp.dot(mixed.astype(_BF16), w_out_ref[...], preferred_element_type=_F32)


def _mlp_kernel(h_ref, g_ref, w_up_ref, w_down_ref, gf_ref, o_ref, *, ff_chunk, final_norm):
    h = h_ref[...]
    ub = _rms_norm(h, g_ref[...]).astype(_BF16)
    d_ff = w_up_ref.shape[1]
    acc = h
    for c in range(0, d_ff, ff_chunk):
        a = jnp.maximum(jnp.dot(ub, w_up_ref[:, c:c + ff_chunk], preferred_element_type=_F32), 0.0)
        acc = acc + jnp.dot((a * a).astype(_BF16), w_down_ref[c:c + ff_chunk, :], preferred_element_type=_F32)
    if final_norm:
        acc = _rms_norm(acc, gf_ref[...])
    o_ref[...] = acc


def _resident(shape):
    zeros = (0,) * len(shape)
    return pl.BlockSpec(shape, lambda *_: zeros, pipeline_mode=pl.Buffered(1))


def _mixer(x, sinks, g, w_in, pool_w, pool_b, pool_scale, bias, p_pool, p_attn, w_out):
    b, s, d = x.shape
    tm = MIXER_TILE
    assert s % tm == 0 and tm % BLOCK == 0
    c_pool = len(POOL_WINDOWS) * POOL_GROUP_DIM
    grid_spec = pltpu.PrefetchScalarGridSpec(
        num_scalar_prefetch=1,
        grid=(b, s // tm),
        in_specs=[
            pl.BlockSpec((None, tm, d), lambda bi, i, _: (bi, i, 0)),
            _resident(g.shape), _resident(w_in.shape), _resident(pool_w.shape),
            _resident(pool_b.shape), _resident(pool_scale.shape), _resident(bias.shape),
            _resident(p_pool.shape), _resident(p_attn.shape), _resident(w_out.shape),
        ],
        out_specs=pl.BlockSpec((None, tm, d), lambda bi, i, _: (bi, i, 0)),
        scratch_shapes=[
            pltpu.VMEM((tm + POOL_HALO, c_pool), _F32),
            pltpu.VMEM((3, tm + POOL_HALO, c_pool), _F32),
            pltpu.VMEM((c_pool, c_pool), _BF16),
            pltpu.VMEM((N_KV_HEADS, tm + BLOCK, LANES), _BF16),
            pltpu.VMEM((N_KV_HEADS, LANES, tm + BLOCK), _BF16),
        ],
    )
    return pl.pallas_call(
        functools.partial(_mixer_kernel, tm=tm),
        grid_spec=grid_spec,
        out_shape=jax.ShapeDtypeStruct(x.shape, _F32),
        compiler_params=pltpu.CompilerParams(dimension_semantics=("arbitrary", "arbitrary")),
        name="mixer",
    )(sinks, x, g, w_in, pool_w, pool_b, pool_scale, bias, p_pool, p_attn, w_out)


def _mlp(h, g, w_up, w_down, g_final, final_norm):
    t, d = h.shape
    tm = MLP_TILE
    assert t % tm == 0
    return pl.pallas_call(
        functools.partial(_mlp_kernel, ff_chunk=1024, final_norm=final_norm),
        grid=(t // tm,),
        in_specs=[
            pl.BlockSpec((tm, d), lambda i: (i, 0)),
            _resident(g.shape), _resident(w_up.shape), _resident(w_down.shape), _resident(g_final.shape),
        ],
        out_specs=pl.BlockSpec((tm, d), lambda i: (i, 0)),
        out_shape=jax.ShapeDtypeStruct(h.shape, _F32),
        compiler_params=pltpu.CompilerParams(dimension_semantics=("arbitrary",)),
        name="mlp",
    )(h, g, w_up, w_down, g_final)


def kernel(x, norm_mix, w_in, pool_w, pool_b, pool_scale, attn_sinks, p_pool, p_attn,
           w_out, norm_mlp, w_up, w_down, norm_final):
    depth = norm_mix.shape[0]
    b, s, d = x.shape
    bias = jnp.asarray(_attention_bias())
    h = x
    for l in range(depth):
        h = _mixer(h, attn_sinks[l], norm_mix[l][None, :], w_in[l].astype(_BF16),
                   pool_w[l].astype(_BF16), pool_b[l].reshape(1, -1), pool_scale[l][None, :], bias,
                   p_pool[l].astype(_BF16), p_attn[l].astype(_BF16), w_out[l].astype(_BF16))
        h = _mlp(h.reshape(b * s, d), norm_mlp[l][None, :], w_up[l].astype(_BF16),
                 w_down[l].astype(_BF16), norm_final[None, :], l == depth - 1).reshape(b, s, d)
    return h
```

```python
import functools

import jax
import jax.numpy as jnp
import numpy as np
from jax import lax
from jax.experimental import pallas as pl
from jax.experimental.pallas import tpu as pltpu

HEAD_DIM = 64
N_KV_HEADS = 2
GROUP = 4
BLOCK = 128
POOL_WINDOWS = (2, 4, 8, 16)
POOL_GROUP_DIM = 128
POOL_HALO = 32
RMS_EPS = 1e-5
NEG_INF = -1e30
LANES = 128

MIXER_TILE = 1024
MLP_TILE = 1024

_F32 = jnp.float32
_BF16 = jnp.bfloat16


def _rms_norm(x, g):
    return x * lax.rsqrt(jnp.mean(x * x, axis=-1, keepdims=True) + RMS_EPS) * g


def _attention_bias():
    kj = np.arange(2 * BLOCK)[:, None]
    qi = np.arange(BLOCK)[None, :]
    dist = (BLOCK + qi - kj).astype(np.float32)
    valid = (dist >= 0) & (dist < BLOCK)
    n_heads = N_KV_HEADS * GROUP
    out = np.empty((N_KV_HEADS, 2 * BLOCK, GROUP * BLOCK), np.float32)
    for h in range(N_KV_HEADS):
        for g in range(GROUP):
            slope = np.float32(2.0 ** (-8.0 * (h * GROUP + g + 1) / n_heads))
            out[h, :, g * BLOCK:(g + 1) * BLOCK] = np.where(valid, -(slope * dist), np.float32(NEG_INF))
    return out


def _mixer_kernel(sinks_ref, x_ref, g_ref, w_in_ref, pool_w_ref, pool_b_ref, pool_scale_ref,
                  bias_ref, p_pool_ref, p_attn_ref, w_out_ref, w_up_ref, w_down_ref,
                  o_ref, w_up_o_ref, w_down_o_ref,
                  pool_hist, pool_stage, pool_bd, k_ext, vt_ext, q_lhs, gates, attn_buf, *, tm):
    i = pl.program_id(1)
    n_groups = len(POOL_WINDOWS)
    c_pool = n_groups * POOL_GROUP_DIM
    c_q = c_pool + N_KV_HEADS * GROUP * HEAD_DIM
    c_v = c_q + 2 * N_KV_HEADS * HEAD_DIM
    d_model = x_ref.shape[-1]
    n_blocks = tm // BLOCK

    @pl.when(i == 0)
    def _():
        pool_hist[0:POOL_HALO, :] = jnp.zeros((POOL_HALO, c_pool), _F32)
        k_ext[:, 0:BLOCK, :] = jnp.zeros((N_KV_HEADS, BLOCK, LANES), _BF16)
        vt_ext[:, :, 0:BLOCK] = jnp.zeros((N_KV_HEADS, LANES, BLOCK), _BF16)
        pool_bd[...] = jnp.zeros((c_pool, c_pool), _BF16)
        for g in range(n_groups):
            lo = g * POOL_GROUP_DIM
            pool_bd[lo:lo + POOL_GROUP_DIM, lo:lo + POOL_GROUP_DIM] = pool_w_ref[g]

    w_up_o_ref[...] = w_up_ref[...].astype(_BF16)
    w_down_o_ref[...] = w_down_ref[...].astype(_BF16)

    x = x_ref[...]
    ub = _rms_norm(x, g_ref[...]).astype(_BF16)

    kv = jnp.dot(ub, w_in_ref[:, c_q:c_v], preferred_element_type=_F32)
    q = jnp.dot(ub, w_in_ref[:, c_pool:c_q], preferred_element_type=_F32) * (1.0 / float(np.sqrt(HEAD_DIM)))
    low = lax.broadcasted_iota(jnp.int32, (tm, LANES), 1) < HEAD_DIM
    k = kv[:, 0:LANES]
    k_swapped = pltpu.roll(k, HEAD_DIM, axis=1)
    k_ext[0, BLOCK:BLOCK + tm, :] = jnp.where(low, k, k_swapped).astype(_BF16)
    k_ext[1, BLOCK:BLOCK + tm, :] = jnp.where(low, k_swapped, k).astype(_BF16)
    vt = kv[:, LANES:2 * LANES].T.astype(_BF16)
    for h in range(N_KV_HEADS):
        vh = vt[h * HEAD_DIM:(h + 1) * HEAD_DIM]
        vt_ext[h, :, BLOCK:BLOCK + tm] = jnp.concatenate([vh, vh], axis=0)
    zeros_t = jnp.zeros((tm, LANES), _F32)
    for c in range(N_KV_HEADS * GROUP // 2):
        h, pair = divmod(c, GROUP // 2)
        qc = q[:, c * LANES:(c + 1) * LANES]
        halves = (jnp.where(low, qc, zeros_t).astype(_BF16), jnp.where(low, zeros_t, qc).astype(_BF16))
        for j in range(n_blocks):
            for e in range(2):
                r = (j * GROUP + 2 * pair + e) * BLOCK
                q_lhs[h, r:r + BLOCK, :] = halves[e][j * BLOCK:(j + 1) * BLOCK]

    up = jnp.dot(ub, w_in_ref[:, 0:c_pool], preferred_element_type=_F32)
    rows = tm + POOL_HALO
    pool_hist[POOL_HALO:rows, :] = up
    pool_stage[0, 8:rows, :] = pool_hist[8:rows, :] + pool_hist[7:rows - 1, :]
    pool_stage[1, 16:rows, 128:] = pool_stage[0, 16:rows, 128:] + pool_stage[0, 14:rows - 2, 128:]
    pool_stage[2, 24:rows, 256:] = pool_stage[1, 24:rows, 256:] + pool_stage[1, 20:rows - 4, 256:]
    s16 = pool_stage[2, 32:rows, 384:] + pool_stage[2, 24:rows - 8, 384:]
    wsum = jnp.concatenate([pool_stage[0, 32:rows, 0:128], pool_stage[1, 32:rows, 128:256],
                            pool_stage[2, 32:rows, 256:384], s16], axis=1)
    t1 = i * tm + lax.broadcasted_iota(jnp.int32, (tm, c_pool), 0) + 1
    win = jnp.left_shift(2, lax.broadcasted_iota(jnp.int32, (tm, c_pool), 1) // POOL_GROUP_DIM)
    count = jnp.minimum(t1, win).astype(_F32)
    d = (wsum / count - up).astype(_BF16)
    pool_hist[0:POOL_HALO, :] = pool_hist[tm:rows, :]

    low_b = lax.broadcasted_iota(jnp.int32, (BLOCK, LANES), 1) < HEAD_DIM
    krow = lax.broadcasted_iota(jnp.int32, (2 * BLOCK, GROUP * BLOCK), 0)
    pad_keys = jnp.logical_and(krow < BLOCK, i == 0)
    sinks = [jnp.concatenate([jnp.full((1, BLOCK), sinks_ref[h * GROUP + g], _F32) for g in range(GROUP)], axis=1)
             for h in range(N_KV_HEADS)]

    def scores(h, j):
        keys = k_ext[h, j * BLOCK:(j + 2) * BLOCK, :]
        lhs = q_lhs[h, j * GROUP * BLOCK:(j + 1) * GROUP * BLOCK, :]
        return lax.dot_general(keys, lhs, (((1,), (1,)), ((), ())), preferred_element_type=_F32)

    def probs(h, j, s):
        s = s + bias_ref[h]
        if j == 0:
            s = jnp.where(pad_keys, NEG_INF, s)
        m = jnp.maximum(jnp.max(s, axis=0, keepdims=True), sinks[h])
        p = jnp.exp(s - m)
        denom = jnp.sum(p, axis=0, keepdims=True) + jnp.exp(sinks[h] - m)
        return p.astype(_BF16), 1.0 / denom

    def weighted_values(h, j, p, inv):
        r0 = j * BLOCK
        ot = jnp.dot(vt_ext[h, :, r0:r0 + 2 * BLOCK], p, preferred_element_type=_F32) * inv
        o = ot.T
        for pair in range(GROUP // 2):
            c0 = (h * (GROUP // 2) + pair) * LANES
            attn_buf[r0:r0 + BLOCK, c0:c0 + LANES] = jnp.where(
                low_b, o[2 * pair * BLOCK:(2 * pair + 1) * BLOCK],
                o[(2 * pair + 1) * BLOCK:(2 * pair + 2) * BLOCK]).astype(_BF16)

    def gate_piece(which, r0, nr, c0, nc):
        col = c_v + which * d_model + c0
        z = jnp.dot(ub[r0:r0 + nr], w_in_ref[:, col:col + nc], preferred_element_type=_F32)
        gates[which, r0:r0 + nr, c0:c0 + nc] = jax.nn.sigmoid(z)

    steps = [(h, j) for h in range(N_KV_HEADS) for j in range(n_blocks)]
    piece_rows, piece_cols = min(tm, 512), 2 * LANES
    pieces = [(which, r0, piece_rows, c0, piece_cols)
              for c0 in range(0, d_model, piece_cols) for r0 in range(0, tm, piece_rows) for which in range(2)]
    lookahead = 2
    pending = [scores(*st) for st in steps[:lookahead]]
    for n, (h, j) in enumerate(steps):
        for piece in pieces[n * len(pieces) // len(steps):(n + 1) * len(pieces) // len(steps)]:
            gate_piece(*piece)
        p, inv = probs(h, j, pending.pop(0))
        if n + lookahead < len(steps):
            pending.append(scores(*steps[n + lookahead]))
        weighted_values(h, j, p, inv)
    k_ext[:, 0:BLOCK, :] = k_ext[:, tm:tm + BLOCK, :]
    vt_ext[:, :, 0:BLOCK] = vt_ext[:, :, tm:tm + BLOCK]

    y = jnp.dot(d, pool_bd[...], preferred_element_type=_F32)
    y_attn = jnp.dot(attn_buf[...], p_attn_ref[...], preferred_element_type=_F32)
    y = ((y + pool_b_ref[...]) * pool_scale_ref[...]).astype(_BF16)
    y_pool = jnp.dot(y, p_pool_ref[...], preferred_element_type=_F32)
    mixed = gates[0] * y_pool + gates[1] * y_attn
    o_ref[...] = x + jnp.dot(mixed.astype(_BF16), w_out_ref[...], preferred_element_type=_F32)


def _mlp_kernel(h_ref, g_ref, w_up_ref, w_down_ref, gf_ref, o_ref, *, ff_chunk, final_norm):
    h = h_ref[...]
    ub = _rms_norm(h, g_ref[...]).astype(_BF16)
    d_ff = w_up_ref.shape[1]
    acc = h
    for c in range(0, d_ff, ff_chunk):
        a = jnp.maximum(jnp.dot(ub, w_up_ref[:, c:c + ff_chunk], preferred_element_type=_F32), 0.0)
        acc = acc + jnp.dot((a * a).astype(_BF16), w_down_ref[c:c + ff_chunk, :], preferred_element_type=_F32)
    if final_norm:
        acc = _rms_norm(acc, gf_ref[...])
    o_ref[...] = acc


def _resident(shape):
    zeros = (0,) * len(shape)
    return pl.BlockSpec(shape, lambda *_: zeros, pipeline_mode=pl.Buffered(1))


def _mixer(x, sinks, g, w_in, pool_w, pool_b, pool_scale, bias, p_pool, p_attn, w_out, w_up, w_down):
    b, s, d = x.shape
    tm = MIXER_TILE
    assert s % tm == 0 and tm % BLOCK == 0
    n_i = s // tm
    d_ff = w_up.shape[1]
    assert d_ff % (b * n_i * LANES) == 0
    ff_slice = d_ff // (b * n_i)
    c_pool = len(POOL_WINDOWS) * POOL_GROUP_DIM
    attn_width = N_KV_HEADS * GROUP * HEAD_DIM
    grid_spec = pltpu.PrefetchScalarGridSpec(
        num_scalar_prefetch=1,
        grid=(b, n_i),
        in_specs=[
            pl.BlockSpec((None, tm, d), lambda bi, i, _: (bi, i, 0)),
            _resident(g.shape), _resident(w_in.shape), _resident(pool_w.shape),
            _resident(pool_b.shape), _resident(pool_scale.shape), _resident(bias.shape),
            _resident(p_pool.shape), _resident(p_attn.shape), _resident(w_out.shape),
            pl.BlockSpec((d, ff_slice), lambda bi, i, _: (0, bi * n_i + i)),
            pl.BlockSpec((ff_slice, d), lambda bi, i, _: (bi * n_i + i, 0)),
        ],
        out_specs=[
            pl.BlockSpec((None, tm, d), lambda bi, i, _: (bi, i, 0)),
            pl.BlockSpec((d, ff_slice), lambda bi, i, _: (0, bi * n_i + i)),
            pl.BlockSpec((ff_slice, d), lambda bi, i, _: (bi * n_i + i, 0)),
        ],
        scratch_shapes=[
            pltpu.VMEM((tm + POOL_HALO, c_pool), _F32),
            pltpu.VMEM((3, tm + POOL_HALO, c_pool), _F32),
            pltpu.VMEM((c_pool, c_pool), _BF16),
            pltpu.VMEM((N_KV_HEADS, tm + BLOCK, LANES), _BF16),
            pltpu.VMEM((N_KV_HEADS, LANES, tm + BLOCK), _BF16),
            pltpu.VMEM((N_KV_HEADS, GROUP * tm, LANES), _BF16),
            pltpu.VMEM((2, tm, d), _F32),
            pltpu.VMEM((tm, attn_width), _BF16),
        ],
    )
    return pl.pallas_call(
        functools.partial(_mixer_kernel, tm=tm),
        grid_spec=grid_spec,
        out_shape=[jax.ShapeDtypeStruct(x.shape, _F32),
                   jax.ShapeDtypeStruct(w_up.shape, _BF16), jax.ShapeDtypeStruct(w_down.shape, _BF16)],
        compiler_params=pltpu.CompilerParams(dimension_semantics=("arbitrary", "arbitrary")),
        name="mixer",
    )(sinks, x, g, w_in, pool_w, pool_b, pool_scale, bias, p_pool, p_attn, w_out, w_up, w_down)


def _mlp(h, g, w_up, w_down, g_final, final_norm):
    t, d = h.shape
    tm = MLP_TILE
    assert t % tm == 0
    return pl.pallas_call(
        functools.partial(_mlp_kernel, ff_chunk=1024, final_norm=final_norm),
        grid=(t // tm,),
        in_specs=[
            pl.BlockSpec((tm, d), lambda i: (i, 0)),
            _resident(g.shape), _resident(w_up.shape), _resident(w_down.shape), _resident(g_final.shape),
        ],
        out_specs=pl.BlockSpec((tm, d), lambda i: (i, 0)),
        out_shape=jax.ShapeDtypeStruct(h.shape, _F32),
        compiler_params=pltpu.CompilerParams(dimension_semantics=("arbitrary",)),
        name="mlp",
    )(h, g, w_up, w_down, g_final)


def kernel(x, norm_mix, w_in, pool_w, pool_b, pool_scale, attn_sinks, p_pool, p_attn,
           w_out, norm_mlp, w_up, w_down, norm_final):
    depth = norm_mix.shape[0]
    b, s, d = x.shape
    bias = jnp.asarray(_attention_bias())
    h = x
    for l in range(depth):
        h, w_up_b, w_down_b = _mixer(
            h, attn_sinks[l], norm_mix[l][None, :], w_in[l].astype(_BF16), pool_w[l].astype(_BF16),
            pool_b[l].reshape(1, -1), pool_scale[l][None, :], bias, p_pool[l].astype(_BF16),
            p_attn[l].astype(_BF16), w_out[l].astype(_BF16), w_up[l], w_down[l])
        h = _mlp(h.reshape(b * s, d), norm_mlp[l][None, :], w_up_b, w_down_b,
                 norm_final[None, :], l == depth - 1).reshape(b, s, d)
    return h
```

```python
import functools

import jax
import jax.numpy as jnp
import numpy as np
from jax import lax
from jax.experimental import pallas as pl
from jax.experimental.pallas import tpu as pltpu

HEAD_DIM = 64
N_KV_HEADS = 2
GROUP = 4
BLOCK = 128
POOL_WINDOWS = (2, 4, 8, 16)
POOL_GROUP_DIM = 128
POOL_HALO = 32
RMS_EPS = 1e-5
NEG_INF = -1e30
LANES = 128

MIXER_TILE = 1024
MLP_TILE = 1024

_F32 = jnp.float32
_BF16 = jnp.bfloat16


def _rms_norm(x, g):
    return x * lax.rsqrt(jnp.mean(x * x, axis=-1, keepdims=True) + RMS_EPS) * g


def _attention_bias():
    kj = np.arange(2 * BLOCK)[:, None]
    qi = np.arange(BLOCK)[None, :]
    dist = (BLOCK + qi - kj).astype(np.float32)
    valid = (dist >= 0) & (dist < BLOCK)
    n_heads = N_KV_HEADS * GROUP
    out = np.empty((N_KV_HEADS, 2 * BLOCK, GROUP * BLOCK), np.float32)
    for h in range(N_KV_HEADS):
        for g in range(GROUP):
            slope = np.float32(2.0 ** (-8.0 * (h * GROUP + g + 1) / n_heads))
            out[h, :, g * BLOCK:(g + 1) * BLOCK] = np.where(valid, -(slope * dist), np.float32(NEG_INF))
    return out


def _mixer_kernel(sinks_ref, x_ref, g_ref, w_in_ref, wq_t_ref, pool_w_ref, pool_b_ref, pool_scale_ref,
                  bias_ref, p_pool_ref, p_attn_ref, w_out_ref, w_up_ref, w_down_ref,
                  o_ref, w_up_o_ref, w_down_o_ref,
                  pool_hist, pool_stage, pool_bd, k_ext, vt_ext, q_t, gates, attn_t, *, tm):
    i = pl.program_id(1)
    n_groups = len(POOL_WINDOWS)
    c_pool = n_groups * POOL_GROUP_DIM
    c_q = c_pool + N_KV_HEADS * GROUP * HEAD_DIM
    c_v = c_q + 2 * N_KV_HEADS * HEAD_DIM
    d_model = x_ref.shape[-1]
    n_blocks = tm // BLOCK
    pair = 2 * POOL_GROUP_DIM

    @pl.when(i == 0)
    def _():
        pool_hist[0:POOL_HALO, :] = jnp.zeros((POOL_HALO, c_pool), _F32)
        k_ext[0:BLOCK, :] = jnp.zeros((BLOCK, LANES), _BF16)
        vt_ext[:, 0:BLOCK] = jnp.zeros((LANES, BLOCK), _BF16)
        pool_bd[...] = jnp.zeros(pool_bd.shape, _BF16)
        for g in range(n_groups):
            lo = (g % 2) * POOL_GROUP_DIM
            pool_bd[g // 2, lo:lo + POOL_GROUP_DIM, lo:lo + POOL_GROUP_DIM] = pool_w_ref[g]

    ub = jnp.concatenate([_rms_norm(x_ref[r:r + BLOCK, :], g_ref[...]).astype(_BF16)
                          for r in range(0, tm, BLOCK)], axis=0)

    kv = jnp.dot(ub, w_in_ref[:, c_q:c_v], preferred_element_type=_F32)
    q_t[...] = (lax.dot_general(wq_t_ref[...], ub, (((1,), (1,)), ((), ())), preferred_element_type=_F32)
                * (1.0 / float(np.sqrt(HEAD_DIM)))).astype(_BF16)
    k_ext[BLOCK:BLOCK + tm, :] = kv[:, 0:LANES].astype(_BF16)
    vt_ext[:, BLOCK:BLOCK + tm] = kv[:, LANES:2 * LANES].T.astype(_BF16)

    krow = lax.broadcasted_iota(jnp.int32, (2 * BLOCK, GROUP * BLOCK), 0)
    pad_keys = jnp.logical_and(krow < BLOCK, i == 0)
    sinks = [jnp.concatenate([jnp.full((1, BLOCK), sinks_ref[h * GROUP + g], _F32) for g in range(GROUP)], axis=1)
             for h in range(N_KV_HEADS)]
    no_dims = jnp.zeros((HEAD_DIM, GROUP * BLOCK), _BF16)

    def scores(h, j):
        keys = k_ext[j * BLOCK:(j + 2) * BLOCK, :]
        q_heads = jnp.concatenate(
            [q_t[(h * GROUP + g) * HEAD_DIM:(h * GROUP + g + 1) * HEAD_DIM, j * BLOCK:(j + 1) * BLOCK]
             for g in range(GROUP)], axis=1)
        rhs = jnp.concatenate([q_heads if hh == h else no_dims for hh in range(N_KV_HEADS)], axis=0)
        return jnp.dot(keys, rhs, preferred_element_type=_F32)

    def probs(h, j, s):
        s = s + bias_ref[h]
        if j == 0:
            s = jnp.where(pad_keys, NEG_INF, s)
        m = jnp.maximum(jnp.max(s, axis=0, keepdims=True), sinks[h])
        p = jnp.exp(s - m)
        denom = jnp.sum(p, axis=0, keepdims=True) + jnp.exp(sinks[h] - m)
        return p.astype(_BF16), 1.0 / denom

    def weighted_values(h, j, p, inv):
        r0 = j * BLOCK
        vals = vt_ext[h * HEAD_DIM:(h + 1) * HEAD_DIM, r0:r0 + 2 * BLOCK]
        ot = (jnp.dot(vals, p, preferred_element_type=_F32) * inv).astype(_BF16)
        for g in range(GROUP):
            d0 = (h * GROUP + g) * HEAD_DIM
            attn_t[d0:d0 + HEAD_DIM, r0:r0 + BLOCK] = ot[:, g * BLOCK:(g + 1) * BLOCK]

    def gate_piece(which, r0, nr, c0, nc):
        col = c_v + which * d_model + c0
        z = jnp.dot(ub[r0:r0 + nr], w_in_ref[:, col:col + nc], preferred_element_type=_F32)
        gates[which, r0:r0 + nr, c0:c0 + nc] = jax.nn.sigmoid(z)

    steps = [(h, j) for h in range(N_KV_HEADS) for j in range(n_blocks)]
    piece_rows, piece_cols = min(tm, 512), 2 * LANES
    pieces = [(which, r0, piece_rows, c0, piece_cols)
              for c0 in range(0, d_model, piece_cols) for r0 in range(0, tm, piece_rows) for which in range(2)]
    lookahead = 2
    pending = [scores(*st) for st in steps[:lookahead]]

    rows = tm + POOL_HALO
    pool_hist[POOL_HALO:rows, :] = jnp.dot(ub, w_in_ref[:, 0:c_pool], preferred_element_type=_F32)

    for n, (h, j) in enumerate(steps):
        p, inv = probs(h, j, pending.pop(0))
        if n + lookahead < len(steps):
            pending.append(scores(*steps[n + lookahead]))
        for piece in pieces[n * len(pieces) // len(steps):(n + 1) * len(pieces) // len(steps)]:
            gate_piece(*piece)
        weighted_values(h, j, p, inv)
    k_ext[0:BLOCK, :] = k_ext[tm:tm + BLOCK, :]
    vt_ext[:, 0:BLOCK] = vt_ext[:, tm:tm + BLOCK]

    y_attn = lax.dot_general(attn_t[...], p_attn_ref[...], (((0,), (0,)), ((), ())),
                             preferred_element_type=_F32)

    w_up_o_ref[...] = w_up_ref[...].astype(_BF16)
    w_down_o_ref[...] = w_down_ref[...].astype(_BF16)

    pool_stage[0, 8:rows, :] = pool_hist[8:rows, :] + pool_hist[7:rows - 1, :]
    pool_stage[1, 16:rows, 128:] = pool_stage[0, 16:rows, 128:] + pool_stage[0, 14:rows - 2, 128:]
    pool_stage[2, 24:rows, 256:] = pool_stage[1, 24:rows, 256:] + pool_stage[1, 20:rows - 4, 256:]
    s16 = pool_stage[2, 32:rows, 384:] + pool_stage[2, 24:rows - 8, 384:]
    wsum = jnp.concatenate([pool_stage[0, 32:rows, 0:128], pool_stage[1, 32:rows, 128:256],
                            pool_stage[2, 32:rows, 256:384], s16], axis=1)
    t1 = i * tm + lax.broadcasted_iota(jnp.int32, (tm, c_pool), 0) + 1
    win = jnp.left_shift(2, lax.broadcasted_iota(jnp.int32, (tm, c_pool), 1) // POOL_GROUP_DIM)
    count = jnp.minimum(t1, win).astype(_F32)
    d = (wsum / count - pool_hist[POOL_HALO:rows, :]).astype(_BF16)
    pool_hist[0:POOL_HALO, :] = pool_hist[tm:rows, :]

    y = jnp.concatenate([jnp.dot(d[:, c * pair:(c + 1) * pair], pool_bd[c], preferred_element_type=_F32)
                         for c in range(n_groups // 2)], axis=1)
    y = ((y + pool_b_ref[...]) * pool_scale_ref[...]).astype(_BF16)
    y_pool = jnp.dot(y, p_pool_ref[...], preferred_element_type=_F32)
    nr = tm // 4
    for r0 in range(0, tm, nr):
        mixed = gates[0, r0:r0 + nr] * y_pool[r0:r0 + nr] + gates[1, r0:r0 + nr] * y_attn[r0:r0 + nr]
        o_ref[r0:r0 + nr, :] = x_ref[r0:r0 + nr, :] + jnp.dot(
            mixed.astype(_BF16), w_out_ref[...], preferred_element_type=_F32)


def _mlp_kernel(h_ref, g_ref, w_up_ref, w_down_ref, gf_ref, o_ref, *, ff_chunk, final_norm):
    d_ff = w_up_ref.shape[1]
    tm = h_ref.shape[0]
    ub = jnp.concatenate([_rms_norm(h_ref[r:r + BLOCK, :], g_ref[...]).astype(_BF16)
                          for r in range(0, tm, BLOCK)], axis=0)
    acc = h_ref[...]
    for c in range(0, d_ff, ff_chunk):
        a = jnp.maximum(jnp.dot(ub, w_up_ref[:, c:c + ff_chunk], preferred_element_type=_F32), 0.0)
        acc = acc + jnp.dot((a * a).astype(_BF16), w_down_ref[c:c + ff_chunk, :], preferred_element_type=_F32)
    if final_norm:
        acc = _rms_norm(acc, gf_ref[...])
    o_ref[...] = acc


def _resident(shape):
    zeros = (0,) * len(shape)
    return pl.BlockSpec(shape, lambda *_: zeros, pipeline_mode=pl.Buffered(1))


def _mixer(x, sinks, g, w_in, wq_t, pool_w, pool_b, pool_scale, bias, p_pool, p_attn, w_out, w_up, w_down):
    b, s, d = x.shape
    tm = MIXER_TILE
    assert s % tm == 0 and tm % BLOCK == 0
    n_i = s // tm
    d_ff = w_up.shape[1]
    assert d_ff % (b * n_i * LANES) == 0
    ff_slice = d_ff // (b * n_i)
    c_pool = len(POOL_WINDOWS) * POOL_GROUP_DIM
    attn_width = N_KV_HEADS * GROUP * HEAD_DIM
    grid_spec = pltpu.PrefetchScalarGridSpec(
        num_scalar_prefetch=1,
        grid=(b, n_i),
        in_specs=[
            pl.BlockSpec((None, tm, d), lambda bi, i, _: (bi, i, 0)),
            _resident(g.shape), _resident(w_in.shape), _resident(wq_t.shape), _resident(pool_w.shape),
            _resident(pool_b.shape), _resident(pool_scale.shape), _resident(bias.shape),
            _resident(p_pool.shape), _resident(p_attn.shape), _resident(w_out.shape),
            pl.BlockSpec((d, ff_slice), lambda bi, i, _: (0, bi * n_i + i)),
            pl.BlockSpec((ff_slice, d), lambda bi, i, _: (bi * n_i + i, 0)),
        ],
        out_specs=[
            pl.BlockSpec((None, tm, d), lambda bi, i, _: (bi, i, 0)),
            pl.BlockSpec((d, ff_slice), lambda bi, i, _: (0, bi * n_i + i)),
            pl.BlockSpec((ff_slice, d), lambda bi, i, _: (bi * n_i + i, 0)),
        ],
        scratch_shapes=[
            pltpu.VMEM((tm + POOL_HALO, c_pool), _F32),
            pltpu.VMEM((3, tm + POOL_HALO, c_pool), _F32),
            pltpu.VMEM((len(POOL_WINDOWS) // 2, 2 * POOL_GROUP_DIM, 2 * POOL_GROUP_DIM), _BF16),
            pltpu.VMEM((tm + BLOCK, N_KV_HEADS * HEAD_DIM), _BF16),
            pltpu.VMEM((N_KV_HEADS * HEAD_DIM, tm + BLOCK), _BF16),
            pltpu.VMEM((attn_width, tm), _BF16),
            pltpu.VMEM((2, tm, d), _F32),
            pltpu.VMEM((attn_width, tm), _BF16),
        ],
    )
    return pl.pallas_call(
        functools.partial(_mixer_kernel, tm=tm),
        grid_spec=grid_spec,
        out_shape=[jax.ShapeDtypeStruct(x.shape, _F32),
                   jax.ShapeDtypeStruct(w_up.shape, _BF16), jax.ShapeDtypeStruct(w_down.shape, _BF16)],
        compiler_params=pltpu.CompilerParams(dimension_semantics=("arbitrary", "arbitrary")),
        name="mixer",
    )(sinks, x, g, w_in, wq_t, pool_w, pool_b, pool_scale, bias, p_pool, p_attn, w_out, w_up, w_down)


def _mlp(h, g, w_up, w_down, g_final, final_norm):
    t, d = h.shape
    tm = MLP_TILE
    assert t % tm == 0
    return pl.pallas_call(
        functools.partial(_mlp_kernel, ff_chunk=1024, final_norm=final_norm),
        grid=(t // tm,),
        in_specs=[
            pl.BlockSpec((tm, d), lambda i: (i, 0)),
            _resident(g.shape), _resident(w_up.shape), _resident(w_down.shape), _resident(g_final.shape),
        ],
        out_specs=pl.BlockSpec((tm, d), lambda i: (i, 0)),
        out_shape=jax.ShapeDtypeStruct(h.shape, _F32),
        compiler_params=pltpu.CompilerParams(dimension_semantics=("arbitrary",)),
        name="mlp",
    )(h, g, w_up, w_down, g_final)


def kernel(x, norm_mix, w_in, pool_w, pool_b, pool_scale, attn_sinks, p_pool, p_attn,
           w_out, norm_mlp, w_up, w_down, norm_final):
    depth = norm_mix.shape[0]
    b, s, d = x.shape
    bias = jnp.asarray(_attention_bias())
    c_pool = len(POOL_WINDOWS) * POOL_GROUP_DIM
    c_q = c_pool + N_KV_HEADS * GROUP * HEAD_DIM
    h = x
    for l in range(depth):
        w_in_b = w_in[l].astype(_BF16)
        h, w_up_b, w_down_b = _mixer(
            h, attn_sinks[l], norm_mix[l][None, :], w_in_b, w_in_b[:, c_pool:c_q].T, pool_w[l].astype(_BF16),
            pool_b[l].reshape(1, -1), pool_scale[l][None, :], bias, p_pool[l].astype(_BF16),
            p_attn[l].astype(_BF16), w_out[l].astype(_BF16), w_up[l], w_down[l])
        h = _mlp(h.reshape(b * s, d), norm_mlp[l][None, :], w_up_b, w_down_b,
                 norm_final[None, :], l == depth - 1).reshape(b, s, d)
    return h
```

```python
import functools

import jax
import jax.numpy as jnp
import numpy as np
from jax import lax
from jax.experimental import pallas as pl
from jax.experimental.pallas import tpu as pltpu

HEAD_DIM = 64
N_KV_HEADS = 2
GROUP = 4
BLOCK = 128
POOL_WINDOWS = (2, 4, 8, 16)
POOL_GROUP_DIM = 128
POOL_HALO = 32
RMS_EPS = 1e-5
NEG_INF = -1e30
LANES = 128

MIXER_TILE = 1024
MLP_TILE = 1024

_F32 = jnp.float32
_BF16 = jnp.bfloat16


def _rms_norm(x, g):
    return x * lax.rsqrt(jnp.mean(x * x, axis=-1, keepdims=True) + RMS_EPS) * g


def _attention_bias():
    kj = np.arange(2 * BLOCK)[:, None]
    qi = np.arange(BLOCK)[None, :]
    dist = (BLOCK + qi - kj).astype(np.float32)
    valid = (dist >= 0) & (dist < BLOCK)
    n_heads = N_KV_HEADS * GROUP
    out = np.empty((N_KV_HEADS, 2 * BLOCK, GROUP * BLOCK), np.float32)
    for h in range(N_KV_HEADS):
        for g in range(GROUP):
            slope = np.float32(2.0 ** (-8.0 * (h * GROUP + g + 1) / n_heads))
            out[h, :, g * BLOCK:(g + 1) * BLOCK] = np.where(valid, -(slope * dist), np.float32(NEG_INF))
    return out


def _mixer_kernel(sinks_ref, x_ref, g_ref, w_in_ref, pool_w_ref, pool_b_ref, pool_scale_ref,
                  bias_ref, p_pool_ref, p_attn_ref, w_out_ref, w_up_ref, w_down_ref,
                  o_ref, w_up_o_ref, w_down_o_ref,
                  pool_hist, pool_stage, pool_bd, wq_t, k_ext, vt_ext, q_t, gates, attn_t, *, tm):
    i = pl.program_id(1)
    n_groups = len(POOL_WINDOWS)
    c_pool = n_groups * POOL_GROUP_DIM
    c_q = c_pool + N_KV_HEADS * GROUP * HEAD_DIM
    c_v = c_q + 2 * N_KV_HEADS * HEAD_DIM
    d_model = x_ref.shape[-1]
    n_blocks = tm // BLOCK
    pair = 2 * POOL_GROUP_DIM

    @pl.when(i == 0)
    def _():
        pool_hist[0:POOL_HALO, :] = jnp.zeros((POOL_HALO, c_pool), _F32)
        k_ext[0:BLOCK, :] = jnp.zeros((BLOCK, LANES), _BF16)
        vt_ext[:, 0:BLOCK] = jnp.zeros((LANES, BLOCK), _BF16)
        pool_bd[...] = jnp.zeros(pool_bd.shape, _BF16)
        for g in range(n_groups):
            lo = (g % 2) * POOL_GROUP_DIM
            pool_bd[g // 2, lo:lo + POOL_GROUP_DIM, lo:lo + POOL_GROUP_DIM] = pool_w_ref[g]
        wq_t[...] = w_in_ref[:, c_pool:c_q].astype(_F32).T.astype(_BF16)

    ub = jnp.concatenate([_rms_norm(x_ref[r:r + BLOCK, :], g_ref[...]).astype(_BF16)
                          for r in range(0, tm, BLOCK)], axis=0)

    kv = jnp.dot(ub, w_in_ref[:, c_q:c_v], preferred_element_type=_F32)
    q_t[...] = (lax.dot_general(wq_t[...], ub, (((1,), (1,)), ((), ())), preferred_element_type=_F32)
                * (1.0 / float(np.sqrt(HEAD_DIM)))).astype(_BF16)
    k_ext[BLOCK:BLOCK + tm, :] = kv[:, 0:LANES].astype(_BF16)
    vt_ext[:, BLOCK:BLOCK + tm] = kv[:, LANES:2 * LANES].T.astype(_BF16)

    krow = lax.broadcasted_iota(jnp.int32, (2 * BLOCK, GROUP * BLOCK), 0)
    pad_keys = jnp.logical_and(krow < BLOCK, i == 0)
    sinks = [jnp.concatenate([jnp.full((1, BLOCK), sinks_ref[h * GROUP + g], _F32) for g in range(GROUP)], axis=1)
             for h in range(N_KV_HEADS)]
    no_dims = jnp.zeros((HEAD_DIM, GROUP * BLOCK), _BF16)

    def scores(h, j):
        keys = k_ext[j * BLOCK:(j + 2) * BLOCK, :]
        q_heads = jnp.concatenate(
            [q_t[(h * GROUP + g) * HEAD_DIM:(h * GROUP + g + 1) * HEAD_DIM, j * BLOCK:(j + 1) * BLOCK]
             for g in range(GROUP)], axis=1)
        rhs = jnp.concatenate([q_heads if hh == h else no_dims for hh in range(N_KV_HEADS)], axis=0)
        return jnp.dot(keys, rhs, preferred_element_type=_F32)

    def probs(h, j, s):
        s = s + bias_ref[h]
        if j == 0:
            s = jnp.where(pad_keys, NEG_INF, s)
        m = jnp.maximum(jnp.max(s, axis=0, keepdims=True), sinks[h])
        p = jnp.exp(s - m)
        denom = jnp.sum(p, axis=0, keepdims=True) + jnp.exp(sinks[h] - m)
        return p.astype(_BF16), 1.0 / denom

    def weighted_values(h, j, p, inv):
        r0 = j * BLOCK
        vals = vt_ext[h * HEAD_DIM:(h + 1) * HEAD_DIM, r0:r0 + 2 * BLOCK]
        ot = (jnp.dot(vals, p, preferred_element_type=_F32) * inv).astype(_BF16)
        for g in range(GROUP):
            d0 = (h * GROUP + g) * HEAD_DIM
            attn_t[d0:d0 + HEAD_DIM, r0:r0 + BLOCK] = ot[:, g * BLOCK:(g + 1) * BLOCK]

    def gate_piece(which, r0, nr, c0, nc):
        col = c_v + which * d_model + c0
        z = jnp.dot(ub[r0:r0 + nr], w_in_ref[:, col:col + nc], preferred_element_type=_F32)
        gates[which, r0:r0 + nr, c0:c0 + nc] = jax.nn.sigmoid(z)

    steps = [(h, j) for h in range(N_KV_HEADS) for j in range(n_blocks)]
    piece_rows, piece_cols = min(tm, 512), 2 * LANES
    pieces = [(which, r0, piece_rows, c0, piece_cols)
              for c0 in range(0, d_model, piece_cols) for r0 in range(0, tm, piece_rows) for which in range(2)]
    lookahead = 2
    pending = [scores(*st) for st in steps[:lookahead]]

    rows = tm + POOL_HALO
    pool_hist[POOL_HALO:rows, :] = jnp.dot(ub, w_in_ref[:, 0:c_pool], preferred_element_type=_F32)

    for n, (h, j) in enumerate(steps):
        p, inv = probs(h, j, pending.pop(0))
        if n + lookahead < len(steps):
            pending.append(scores(*steps[n + lookahead]))
        for piece in pieces[n * len(pieces) // len(steps):(n + 1) * len(pieces) // len(steps)]:
            gate_piece(*piece)
        weighted_values(h, j, p, inv)
    k_ext[0:BLOCK, :] = k_ext[tm:tm + BLOCK, :]
    vt_ext[:, 0:BLOCK] = vt_ext[:, tm:tm + BLOCK]

    y_attn = lax.dot_general(attn_t[...], p_attn_ref[...], (((0,), (0,)), ((), ())),
                             preferred_element_type=_F32)

    w_up_o_ref[...] = w_up_ref[...].astype(_BF16)
    w_down_o_ref[...] = w_down_ref[...].astype(_BF16)

    pool_stage[0, 8:rows, :] = pool_hist[8:rows, :] + pool_hist[7:rows - 1, :]
    pool_stage[1, 16:rows, 128:] = pool_stage[0, 16:rows, 128:] + pool_stage[0, 14:rows - 2, 128:]
    pool_stage[2, 24:rows, 256:] = pool_stage[1, 24:rows, 256:] + pool_stage[1, 20:rows - 4, 256:]
    s16 = pool_stage[2, 32:rows, 384:] + pool_stage[2, 24:rows - 8, 384:]
    wsum = jnp.concatenate([pool_stage[0, 32:rows, 0:128], pool_stage[1, 32:rows, 128:256],
                            pool_stage[2, 32:rows, 256:384], s16], axis=1)
    t1 = i * tm + lax.broadcasted_iota(jnp.int32, (tm, c_pool), 0) + 1
    win = jnp.left_shift(2, lax.broadcasted_iota(jnp.int32, (tm, c_pool), 1) // POOL_GROUP_DIM)
    count = jnp.minimum(t1, win).astype(_F32)
    d = (wsum / count - pool_hist[POOL_HALO:rows, :]).astype(_BF16)
    pool_hist[0:POOL_HALO, :] = pool_hist[tm:rows, :]

    y = jnp.concatenate([jnp.dot(d[:, c * pair:(c + 1) * pair], pool_bd[c], preferred_element_type=_F32)
                         for c in range(n_groups // 2)], axis=1)
    y = ((y + pool_b_ref[...]) * pool_scale_ref[...]).astype(_BF16)
    y_pool = jnp.dot(y, p_pool_ref[...], preferred_element_type=_F32)
    nr = tm // 4
    for r0 in range(0, tm, nr):
        mixed = gates[0, r0:r0 + nr] * y_pool[r0:r0 + nr] + gates[1, r0:r0 + nr] * y_attn[r0:r0 + nr]
        o_ref[r0:r0 + nr, :] = x_ref[r0:r0 + nr, :] + jnp.dot(
            mixed.astype(_BF16), w_out_ref[...], preferred_element_type=_F32)


def _mlp_kernel(h_ref, g_ref, w_up_ref, w_down_ref, gf_ref, o_ref, *, ff_chunk, final_norm):
    d_ff = w_up_ref.shape[1]
    tm = h_ref.shape[0]
    ub = jnp.concatenate([_rms_norm(h_ref[r:r + BLOCK, :], g_ref[...]).astype(_BF16)
                          for r in range(0, tm, BLOCK)], axis=0)
    acc = h_ref[...]
    for c in range(0, d_ff, ff_chunk):
        a = jnp.maximum(jnp.dot(ub, w_up_ref[:, c:c + ff_chunk], preferred_element_type=_F32), 0.0)
        acc = acc + jnp.dot((a * a).astype(_BF16), w_down_ref[c:c + ff_chunk, :], preferred_element_type=_F32)
    if final_norm:
        acc = _rms_norm(acc, gf_ref[...])
    o_ref[...] = acc


def _resident(shape):
    zeros = (0,) * len(shape)
    return pl.BlockSpec(shape, lambda *_: zeros, pipeline_mode=pl.Buffered(1))


def _mixer(x, sinks, g, w_in, pool_w, pool_b, pool_scale, bias, p_pool, p_attn, w_out, w_up, w_down):
    b, s, d = x.shape
    tm = MIXER_TILE
    assert s % tm == 0 and tm % BLOCK == 0
    n_i = s // tm
    d_ff = w_up.shape[1]
    assert d_ff % (b * n_i * LANES) == 0
    ff_slice = d_ff // (b * n_i)
    c_pool = len(POOL_WINDOWS) * POOL_GROUP_DIM
    attn_width = N_KV_HEADS * GROUP * HEAD_DIM
    grid_spec = pltpu.PrefetchScalarGridSpec(
        num_scalar_prefetch=1,
        grid=(b, n_i),
        in_specs=[
            pl.BlockSpec((None, tm, d), lambda bi, i, _: (bi, i, 0)),
            _resident(g.shape), _resident(w_in.shape), _resident(pool_w.shape),
            _resident(pool_b.shape), _resident(pool_scale.shape), _resident(bias.shape),
            _resident(p_pool.shape), _resident(p_attn.shape), _resident(w_out.shape),
            pl.BlockSpec((d, ff_slice), lambda bi, i, _: (0, bi * n_i + i)),
            pl.BlockSpec((ff_slice, d), lambda bi, i, _: (bi * n_i + i, 0)),
        ],
        out_specs=[
            pl.BlockSpec((None, tm, d), lambda bi, i, _: (bi, i, 0)),
            pl.BlockSpec((d, ff_slice), lambda bi, i, _: (0, bi * n_i + i)),
            pl.BlockSpec((ff_slice, d), lambda bi, i, _: (bi * n_i + i, 0)),
        ],
        scratch_shapes=[
            pltpu.VMEM((tm + POOL_HALO, c_pool), _F32),
            pltpu.VMEM((3, tm + POOL_HALO, c_pool), _F32),
            pltpu.VMEM((len(POOL_WINDOWS) // 2, 2 * POOL_GROUP_DIM, 2 * POOL_GROUP_DIM), _BF16),
            pltpu.VMEM((attn_width, d), _BF16),
            pltpu.VMEM((tm + BLOCK, N_KV_HEADS * HEAD_DIM), _BF16),
            pltpu.VMEM((N_KV_HEADS * HEAD_DIM, tm + BLOCK), _BF16),
            pltpu.VMEM((attn_width, tm), _BF16),
            pltpu.VMEM((2, tm, d), _F32),
            pltpu.VMEM((attn_width, tm), _BF16),
        ],
    )
    return pl.pallas_call(
        functools.partial(_mixer_kernel, tm=tm),
        grid_spec=grid_spec,
        out_shape=[jax.ShapeDtypeStruct(x.shape, _F32),
                   jax.ShapeDtypeStruct(w_up.shape, _BF16), jax.ShapeDtypeStruct(w_down.shape, _BF16)],
        compiler_params=pltpu.CompilerParams(dimension_semantics=("arbitrary", "arbitrary")),
        name="mixer",
    )(sinks, x, g, w_in, pool_w, pool_b, pool_scale, bias, p_pool, p_attn, w_out, w_up, w_down)


def _mlp(h, g, w_up, w_down, g_final, final_norm):
    t, d = h.shape
    tm = MLP_TILE
    assert t % tm == 0
    return pl.pallas_call(
        functools.partial(_mlp_kernel, ff_chunk=1024, final_norm=final_norm),
        grid=(t // tm,),
        in_specs=[
            pl.BlockSpec((tm, d), lambda i: (i, 0)),
            _resident(g.shape), _resident(w_up.shape), _resident(w_down.shape), _resident(g_final.shape),
        ],
        out_specs=pl.BlockSpec((tm, d), lambda i: (i, 0)),
        out_shape=jax.ShapeDtypeStruct(h.shape, _F32),
        compiler_params=pltpu.CompilerParams(dimension_semantics=("arbitrary",)),
        name="mlp",
    )(h, g, w_up, w_down, g_final)


def kernel(x, norm_mix, w_in, pool_w, pool_b, pool_scale, attn_sinks, p_pool, p_attn,
           w_out, norm_mlp, w_up, w_down, norm_final):
    depth = norm_mix.shape[0]
    b, s, d = x.shape
    bias = jnp.asarray(_attention_bias())
    h = x
    for l in range(depth):
        h, w_up_b, w_down_b = _mixer(
            h, attn_sinks[l], norm_mix[l][None, :], w_in[l].astype(_BF16), pool_w[l].astype(_BF16),
            pool_b[l].reshape(1, -1), pool_scale[l][None, :], bias, p_pool[l].astype(_BF16),
            p_attn[l].astype(_BF16), w_out[l].astype(_BF16), w_up[l], w_down[l])
        h = _mlp(h.reshape(b * s, d), norm_mlp[l][None, :], w_up_b, w_down_b,
                 norm_final[None, :], l == depth - 1).reshape(b, s, d)
    return h
```

```python
import functools

import jax
import jax.numpy as jnp
import numpy as np
from jax import lax
from jax.experimental import pallas as pl
from jax.experimental.pallas import tpu as pltpu

HEAD_DIM = 64
N_KV_HEADS = 2
GROUP = 4
BLOCK = 128
POOL_WINDOWS = (2, 4, 8, 16)
POOL_GROUP_DIM = 128
POOL_HALO = 32
RMS_EPS = 1e-5
NEG_INF = -1e30
LANES = 128

MIXER_TILE = 1024
WEIGHT_STAGE_BLOCK = (256, 1024)
WEIGHT_STAGE_SLOTS = 4
MLP_TILE = 1024

_F32 = jnp.float32
_BF16 = jnp.bfloat16


def _rms_norm(x, g):
    return x * lax.rsqrt(jnp.mean(x * x, axis=-1, keepdims=True) + RMS_EPS) * g


def _attention_bias():
    kj = np.arange(2 * BLOCK)[:, None]
    qi = np.arange(BLOCK)[None, :]
    dist = (BLOCK + qi - kj).astype(np.float32)
    valid = (dist >= 0) & (dist < BLOCK)
    n_heads = N_KV_HEADS * GROUP
    out = np.empty((N_KV_HEADS, 2 * BLOCK, GROUP * BLOCK), np.float32)
    for h in range(N_KV_HEADS):
        for g in range(GROUP):
            slope = np.float32(2.0 ** (-8.0 * (h * GROUP + g + 1) / n_heads))
            out[h, :, g * BLOCK:(g + 1) * BLOCK] = np.where(valid, -(slope * dist), np.float32(NEG_INF))
    return out


def _mixer_kernel(sinks_ref, x_ref, g_ref, w_in_hbm, pool_w_ref, pool_b_ref, pool_scale_ref,
                  bias_ref, p_pool_hbm, p_attn_hbm, w_out_hbm, w_up_ref, w_down_ref,
                  o_ref, w_up_o_ref, w_down_o_ref,
                  w_in_ref, p_pool_ref, p_attn_ref, w_out_ref, stage, stage_sem,
                  pool_hist, pool_s2, pool_s4, pool_s8, pool_bd, wq_t, k_ext, vt_ext, q_t, gates, attn_t, *, tm):
    i = pl.program_id(1)
    n_groups = len(POOL_WINDOWS)
    c_pool = n_groups * POOL_GROUP_DIM
    c_q = c_pool + N_KV_HEADS * GROUP * HEAD_DIM
    c_v = c_q + 2 * N_KV_HEADS * HEAD_DIM
    d_model = x_ref.shape[-1]
    n_blocks = tm // BLOCK
    pair = 2 * POOL_GROUP_DIM

    @pl.when(i == 0)
    def _():
        pool_hist[0:POOL_HALO, :] = jnp.zeros((POOL_HALO, c_pool), _F32)
        k_ext[0:BLOCK, :] = jnp.zeros((BLOCK, LANES), _BF16)
        vt_ext[:, 0:BLOCK] = jnp.zeros((LANES, BLOCK), _BF16)

    @pl.when(jnp.logical_and(pl.program_id(0) == 0, i == 0))
    def _():
        srows, scols = stage.shape[1:]
        assert c_q <= scols
        chunks = []
        for src, dst in ((w_in_hbm, w_in_ref), (p_pool_hbm, p_pool_ref), (p_attn_hbm, p_attn_ref),
                         (w_out_hbm, w_out_ref)):
            for r in range(0, src.shape[0], srows):
                for c in range(0, src.shape[1], scols):
                    w = min(scols, src.shape[1] - c)
                    chunks.append((src.at[r:r + srows, c:c + w], dst.at[r:r + srows, c:c + w], r, w,
                                   src is w_in_hbm and c == 0))

        def chunk_copy(n):
            slot = n % stage.shape[0]
            return pltpu.make_async_copy(chunks[n][0], stage.at[slot, :, 0:chunks[n][3]], stage_sem.at[slot])

        for n in range(min(stage.shape[0], len(chunks))):
            chunk_copy(n).start()
        for n, (_, dst, r, w, has_q) in enumerate(chunks):
            chunk_copy(n).wait()
            vals = stage[n % stage.shape[0], :, 0:w]
            dst[...] = vals.astype(_BF16)
            if has_q:
                wq_t[:, r:r + srows] = vals[:, c_pool:c_q].T.astype(_BF16)
            if n + stage.shape[0] < len(chunks):
                chunk_copy(n + stage.shape[0]).start()
        pool_bd[...] = jnp.zeros(pool_bd.shape, _BF16)
        for g in range(n_groups):
            lo = (g % 2) * POOL_GROUP_DIM
            pool_bd[g // 2, lo:lo + POOL_GROUP_DIM, lo:lo + POOL_GROUP_DIM] = pool_w_ref[g].astype(_BF16)

    ub = jnp.concatenate([_rms_norm(x_ref[r:r + BLOCK, :], g_ref[...]).astype(_BF16)
                          for r in range(0, tm, BLOCK)], axis=0)

    kv = jnp.dot(ub, w_in_ref[:, c_q:c_v], preferred_element_type=_F32)
    q_t[...] = (lax.dot_general(wq_t[...], ub, (((1,), (1,)), ((), ())), preferred_element_type=_F32)
                * (1.0 / float(np.sqrt(HEAD_DIM)))).astype(_BF16)
    k_ext[BLOCK:BLOCK + tm, :] = kv[:, 0:LANES].astype(_BF16)
    vt_ext[:, BLOCK:BLOCK + tm] = kv[:, LANES:2 * LANES].T.astype(_BF16)

    krow = lax.broadcasted_iota(jnp.int32, (2 * BLOCK, GROUP * BLOCK), 0)
    pad_keys = jnp.logical_and(krow < BLOCK, i == 0)
    sinks = [jnp.concatenate([jnp.full((1, BLOCK), sinks_ref[h * GROUP + g], _F32) for g in range(GROUP)], axis=1)
             for h in range(N_KV_HEADS)]
    no_dims = jnp.zeros((HEAD_DIM, GROUP * BLOCK), _BF16)

    def scores(h, j):
        keys = k_ext[j * BLOCK:(j + 2) * BLOCK, :]
        q_heads = jnp.concatenate(
            [q_t[(h * GROUP + g) * HEAD_DIM:(h * GROUP + g + 1) * HEAD_DIM, j * BLOCK:(j + 1) * BLOCK]
             for g in range(GROUP)], axis=1)
        rhs = jnp.concatenate([q_heads if hh == h else no_dims for hh in range(N_KV_HEADS)], axis=0)
        return jnp.dot(keys, rhs, preferred_element_type=_F32)

    def probs(h, j, s):
        s = s + bias_ref[h]
        if j == 0:
            s = jnp.where(pad_keys, NEG_INF, s)
        m = jnp.maximum(jnp.max(s, axis=0, keepdims=True), sinks[h])
        p = jnp.exp(s - m)
        denom = jnp.sum(p, axis=0, keepdims=True) + jnp.exp(sinks[h] - m)
        return p.astype(_BF16), 1.0 / denom

    def weighted_values(h, j, p, inv):
        r0 = j * BLOCK
        vals = vt_ext[h * HEAD_DIM:(h + 1) * HEAD_DIM, r0:r0 + 2 * BLOCK]
        ot = (jnp.dot(vals, p, preferred_element_type=_F32) * inv).astype(_BF16)
        for g in range(GROUP):
            d0 = (h * GROUP + g) * HEAD_DIM
            attn_t[d0:d0 + HEAD_DIM, r0:r0 + BLOCK] = ot[:, g * BLOCK:(g + 1) * BLOCK]

    def gate_piece(which, r0, nr, c0, nc):
        col = c_v + which * d_model + c0
        z = jnp.dot(ub[r0:r0 + nr], w_in_ref[:, col:col + nc], preferred_element_type=_F32)
        gates[which, r0:r0 + nr, c0:c0 + nc] = jax.nn.sigmoid(z)

    steps = [(h, j) for h in range(N_KV_HEADS) for j in range(n_blocks)]
    piece_rows, piece_cols = min(tm, 512), 2 * LANES
    pieces = [(which, r0, piece_rows, c0, piece_cols)
              for c0 in range(0, d_model, piece_cols) for r0 in range(0, tm, piece_rows) for which in range(2)]
    lookahead = 2
    pending = [scores(*st) for st in steps[:lookahead]]

    rows = tm + POOL_HALO
    pool_hist[POOL_HALO:rows, :] = jnp.dot(ub, w_in_ref[:, 0:c_pool], preferred_element_type=_F32)

    for n, (h, j) in enumerate(steps):
        p, inv = probs(h, j, pending.pop(0))
        if n + lookahead < len(steps):
            pending.append(scores(*steps[n + lookahead]))
        for piece in pieces[n * len(pieces) // len(steps):(n + 1) * len(pieces) // len(steps)]:
            gate_piece(*piece)
        weighted_values(h, j, p, inv)
    k_ext[0:BLOCK, :] = k_ext[tm:tm + BLOCK, :]
    vt_ext[:, 0:BLOCK] = vt_ext[:, tm:tm + BLOCK]

    y_attn = lax.dot_general(attn_t[...], p_attn_ref[...], (((0,), (0,)), ((), ())),
                             preferred_element_type=_F32)

    w_up_o_ref[...] = w_up_ref[...].astype(_BF16)
    w_down_o_ref[...] = w_down_ref[...].astype(_BF16)

    pool_s2[8:rows, :] = pool_hist[8:rows, :] + pool_hist[7:rows - 1, :]
    pool_s4[16:rows, :] = pool_s2[16:rows, 128:] + pool_s2[14:rows - 2, 128:]
    pool_s8[24:rows, :] = pool_s4[24:rows, 128:] + pool_s4[20:rows - 4, 128:]
    s16 = pool_s8[32:rows, 128:] + pool_s8[24:rows - 8, 128:]
    wsum = jnp.concatenate([pool_s2[32:rows, 0:128], pool_s4[32:rows, 0:128],
                            pool_s8[32:rows, 0:128], s16], axis=1)
    t1 = i * tm + lax.broadcasted_iota(jnp.int32, (tm, c_pool), 0) + 1
    win = jnp.left_shift(2, lax.broadcasted_iota(jnp.int32, (tm, c_pool), 1) // POOL_GROUP_DIM)
    count = jnp.minimum(t1, win).astype(_F32)
    d = (wsum / count - pool_hist[POOL_HALO:rows, :]).astype(_BF16)
    pool_hist[0:POOL_HALO, :] = pool_hist[tm:rows, :]

    y = jnp.concatenate([jnp.dot(d[:, c * pair:(c + 1) * pair], pool_bd[c], preferred_element_type=_F32)
                         for c in range(n_groups // 2)], axis=1)
    y = ((y + pool_b_ref[...]) * pool_scale_ref[...]).astype(_BF16)
    y_pool = jnp.dot(y, p_pool_ref[...], preferred_element_type=_F32)
    nr = tm // 4
    for r0 in range(0, tm, nr):
        mixed = gates[0, r0:r0 + nr] * y_pool[r0:r0 + nr] + gates[1, r0:r0 + nr] * y_attn[r0:r0 + nr]
        o_ref[r0:r0 + nr, :] = x_ref[r0:r0 + nr, :] + jnp.dot(
            mixed.astype(_BF16), w_out_ref[...], preferred_element_type=_F32)


def _mlp_kernel(h_ref, g_ref, w_up_ref, w_down_ref, gf_ref, o_ref, *, ff_chunk, final_norm):
    d_ff = w_up_ref.shape[1]
    tm = h_ref.shape[0]
    ub = jnp.concatenate([_rms_norm(h_ref[r:r + BLOCK, :], g_ref[...]).astype(_BF16)
                          for r in range(0, tm, BLOCK)], axis=0)
    def up(c):
        a = jnp.maximum(jnp.dot(ub, w_up_ref[:, c:c + ff_chunk], preferred_element_type=_F32), 0.0)
        return (a * a).astype(_BF16)

    chunks = list(range(0, d_ff, ff_chunk))
    acc = h_ref[...]
    act = up(chunks[0])
    for n, c in enumerate(chunks):
        nxt = up(chunks[n + 1]) if n + 1 < len(chunks) else None
        acc = acc + jnp.dot(act, w_down_ref[c:c + ff_chunk, :], preferred_element_type=_F32)
        act = nxt
    if final_norm:
        acc = _rms_norm(acc, gf_ref[...])
    o_ref[...] = acc


def _resident(shape):
    zeros = (0,) * len(shape)
    return pl.BlockSpec(shape, lambda *_: zeros, pipeline_mode=pl.Buffered(1))


def _mixer(x, sinks, g, w_in, pool_w, pool_b, pool_scale, bias, p_pool, p_attn, w_out, w_up, w_down):
    b, s, d = x.shape
    tm = MIXER_TILE
    assert s % tm == 0 and tm % BLOCK == 0
    n_i = s // tm
    d_ff = w_up.shape[1]
    assert d_ff % (b * n_i * LANES) == 0
    ff_slice = d_ff // (b * n_i)
    c_pool = len(POOL_WINDOWS) * POOL_GROUP_DIM
    attn_width = N_KV_HEADS * GROUP * HEAD_DIM
    grid_spec = pltpu.PrefetchScalarGridSpec(
        num_scalar_prefetch=1,
        grid=(b, n_i),
        in_specs=[
            pl.BlockSpec((None, tm, d), lambda bi, i, _: (bi, i, 0)),
            _resident(g.shape), pl.BlockSpec(memory_space=pl.ANY), _resident(pool_w.shape),
            _resident(pool_b.shape), _resident(pool_scale.shape), _resident(bias.shape),
            pl.BlockSpec(memory_space=pl.ANY), pl.BlockSpec(memory_space=pl.ANY), pl.BlockSpec(memory_space=pl.ANY),
            pl.BlockSpec((d, ff_slice), lambda bi, i, _: (0, bi * n_i + i)),
            pl.BlockSpec((ff_slice, d), lambda bi, i, _: (bi * n_i + i, 0)),
        ],
        out_specs=[
            pl.BlockSpec((None, tm, d), lambda bi, i, _: (bi, i, 0)),
            pl.BlockSpec((d, ff_slice), lambda bi, i, _: (0, bi * n_i + i)),
            pl.BlockSpec((ff_slice, d), lambda bi, i, _: (bi * n_i + i, 0)),
        ],
        scratch_shapes=[
            pltpu.VMEM(w_in.shape, _BF16), pltpu.VMEM(p_pool.shape, _BF16),
            pltpu.VMEM(p_attn.shape, _BF16), pltpu.VMEM(w_out.shape, _BF16),
            pltpu.VMEM((WEIGHT_STAGE_SLOTS,) + WEIGHT_STAGE_BLOCK, _F32),
            pltpu.SemaphoreType.DMA((WEIGHT_STAGE_SLOTS,)),
            pltpu.VMEM((tm + POOL_HALO, c_pool), _F32),
            pltpu.VMEM((tm + POOL_HALO, c_pool), _F32),
            pltpu.VMEM((tm + POOL_HALO, c_pool - POOL_GROUP_DIM), _F32),
            pltpu.VMEM((tm + POOL_HALO, c_pool - 2 * POOL_GROUP_DIM), _F32),
            pltpu.VMEM((len(POOL_WINDOWS) // 2, 2 * POOL_GROUP_DIM, 2 * POOL_GROUP_DIM), _BF16),
            pltpu.VMEM((attn_width, d), _BF16),
            pltpu.VMEM((tm + BLOCK, N_KV_HEADS * HEAD_DIM), _BF16),
            pltpu.VMEM((N_KV_HEADS * HEAD_DIM, tm + BLOCK), _BF16),
            pltpu.VMEM((attn_width, tm), _BF16),
            pltpu.VMEM((2, tm, d), _F32),
            pltpu.VMEM((attn_width, tm), _BF16),
        ],
    )
    return pl.pallas_call(
        functools.partial(_mixer_kernel, tm=tm),
        grid_spec=grid_spec,
        out_shape=[jax.ShapeDtypeStruct(x.shape, _F32),
                   jax.ShapeDtypeStruct(w_up.shape, _BF16), jax.ShapeDtypeStruct(w_down.shape, _BF16)],
        compiler_params=pltpu.CompilerParams(dimension_semantics=("arbitrary", "arbitrary")),
        name="mixer",
    )(sinks, x, g, w_in, pool_w, pool_b, pool_scale, bias, p_pool, p_attn, w_out, w_up, w_down)


def _mlp(h, g, w_up, w_down, g_final, final_norm):
    t, d = h.shape
    tm = MLP_TILE
    assert t % tm == 0
    return pl.pallas_call(
        functools.partial(_mlp_kernel, ff_chunk=1024, final_norm=final_norm),
        grid=(t // tm,),
        in_specs=[
            pl.BlockSpec((tm, d), lambda i: (i, 0)),
            _resident(g.shape), _resident(w_up.shape), _resident(w_down.shape), _resident(g_final.shape),
        ],
        out_specs=pl.BlockSpec((tm, d), lambda i: (i, 0)),
        out_shape=jax.ShapeDtypeStruct(h.shape, _F32),
        compiler_params=pltpu.CompilerParams(dimension_semantics=("arbitrary",)),
        name="mlp",
    )(h, g, w_up, w_down, g_final)


def kernel(x, norm_mix, w_in, pool_w, pool_b, pool_scale, attn_sinks, p_pool, p_attn,
           w_out, norm_mlp, w_up, w_down, norm_final):
    depth = norm_mix.shape[0]
    b, s, d = x.shape
    bias = jnp.asarray(_attention_bias())
    h = x
    for l in range(depth):
        h, w_up_b, w_down_b = _mixer(
            h, attn_sinks[l], norm_mix[l][None, :], w_in[l], pool_w[l], pool_b[l].reshape(1, -1),
            pool_scale[l][None, :], bias, p_pool[l], p_attn[l], w_out[l], w_up[l], w_down[l])
        h = _mlp(h.reshape(b * s, d), norm_mlp[l][None, :], w_up_b, w_down_b,
                 norm_final[None, :], l == depth - 1).reshape(b, s, d)
    return h
```

```python
import functools

import jax
import jax.numpy as jnp
import numpy as np
from jax import lax
from jax.experimental import pallas as pl
from jax.experimental.pallas import tpu as pltpu

HEAD_DIM = 64
N_KV_HEADS = 2
GROUP = 4
BLOCK = 128
POOL_WINDOWS = (2, 4, 8, 16)
POOL_GROUP_DIM = 128
POOL_HALO = 32
RMS_EPS = 1e-5
NEG_INF = -1e30
LANES = 128

MIXER_TILE = 1024
WEIGHT_STAGE_BLOCK = (256, 1024)
WEIGHT_STAGE_SLOTS = 4
MLP_TILE = 1024

_F32 = jnp.float32
_BF16 = jnp.bfloat16


def _rms_norm(x, g):
    return x * lax.rsqrt(jnp.mean(x * x, axis=-1, keepdims=True) + RMS_EPS) * g


def _attention_bias():
    kj = np.arange(2 * BLOCK)[:, None]
    qi = np.arange(BLOCK)[None, :]
    dist = (BLOCK + qi - kj).astype(np.float32)
    valid = (dist >= 0) & (dist < BLOCK)
    n_heads = N_KV_HEADS * GROUP
    out = np.empty((2 * BLOCK, n_heads * BLOCK), np.float32)
    for head in range(n_heads):
        slope = np.float32(2.0 ** (-8.0 * (head + 1) / n_heads))
        out[:, head * BLOCK:(head + 1) * BLOCK] = np.where(valid, -(slope * dist), np.float32(NEG_INF))
    return out


def _mixer_kernel(sinks_ref, x_ref, g_ref, w_in_hbm, pool_w_ref, pool_b_ref, pool_scale_ref,
                  bias_ref, p_pool_hbm, p_attn_hbm, w_out_hbm, w_up_ref, w_down_ref,
                  o_ref, w_up_o_ref, w_down_o_ref,
                  w_in_ref, p_pool_ref, p_attn_ref, w_out_ref, stage, stage_sem,
                  pool_hist, pool_s2, pool_s4, pool_s8, pool_bd, wq_t, k_ext, vt_ext, q_t, gates, attn_t, *, tm):
    i = pl.program_id(1)
    n_groups = len(POOL_WINDOWS)
    c_pool = n_groups * POOL_GROUP_DIM
    c_q = c_pool + N_KV_HEADS * GROUP * HEAD_DIM
    c_v = c_q + 2 * N_KV_HEADS * HEAD_DIM
    d_model = x_ref.shape[-1]
    n_blocks = tm // BLOCK
    pair = 2 * POOL_GROUP_DIM

    @pl.when(i == 0)
    def _():
        pool_hist[0:POOL_HALO, :] = jnp.zeros((POOL_HALO, c_pool), _F32)
        k_ext[0:BLOCK, :] = jnp.zeros((BLOCK, LANES), _BF16)
        vt_ext[:, 0:BLOCK] = jnp.zeros((LANES, BLOCK), _BF16)

    @pl.when(jnp.logical_and(pl.program_id(0) == 0, i == 0))
    def _():
        srows, scols = stage.shape[1:]
        assert c_q <= scols
        chunks = []
        for src, dst in ((w_in_hbm, w_in_ref), (p_pool_hbm, p_pool_ref), (p_attn_hbm, p_attn_ref),
                         (w_out_hbm, w_out_ref)):
            for r in range(0, src.shape[0], srows):
                for c in range(0, src.shape[1], scols):
                    w = min(scols, src.shape[1] - c)
                    chunks.append((src.at[r:r + srows, c:c + w], dst.at[r:r + srows, c:c + w], r, w,
                                   src is w_in_hbm and c == 0))

        def chunk_copy(n):
            slot = n % stage.shape[0]
            return pltpu.make_async_copy(chunks[n][0], stage.at[slot, :, 0:chunks[n][3]], stage_sem.at[slot])

        for n in range(min(stage.shape[0], len(chunks))):
            chunk_copy(n).start()
        for n, (_, dst, r, w, has_q) in enumerate(chunks):
            chunk_copy(n).wait()
            vals = stage[n % stage.shape[0], :, 0:w]
            dst[...] = vals.astype(_BF16)
            if has_q:
                wq_t[:, r:r + srows] = vals[:, c_pool:c_q].T.astype(_BF16)
            if n + stage.shape[0] < len(chunks):
                chunk_copy(n + stage.shape[0]).start()
        pool_bd[...] = jnp.zeros(pool_bd.shape, _BF16)
        for g in range(n_groups):
            lo = (g % 2) * POOL_GROUP_DIM
            pool_bd[g // 2, lo:lo + POOL_GROUP_DIM, lo:lo + POOL_GROUP_DIM] = pool_w_ref[g].astype(_BF16)

    ub = jnp.concatenate([_rms_norm(x_ref[r:r + BLOCK, :], g_ref[...]).astype(_BF16)
                          for r in range(0, tm, BLOCK)], axis=0)

    kv = jnp.dot(ub, w_in_ref[:, c_q:c_v], preferred_element_type=_F32)
    q_t[...] = (lax.dot_general(wq_t[...], ub, (((1,), (1,)), ((), ())), preferred_element_type=_F32)
                * (1.0 / float(np.sqrt(HEAD_DIM)))).astype(_BF16)
    k_ext[BLOCK:BLOCK + tm, :] = kv[:, 0:LANES].astype(_BF16)
    vt_ext[:, BLOCK:BLOCK + tm] = kv[:, LANES:2 * LANES].T.astype(_BF16)

    n_heads = N_KV_HEADS * GROUP
    krow = lax.broadcasted_iota(jnp.int32, (2 * BLOCK, n_heads * BLOCK), 0)
    pad_keys = jnp.logical_and(krow < BLOCK, i == 0)
    sinks = jnp.concatenate([jnp.full((1, BLOCK), sinks_ref[head], _F32) for head in range(n_heads)], axis=1)
    no_dims = jnp.zeros((HEAD_DIM, GROUP * BLOCK), _BF16)

    def scores(j):
        keys = k_ext[j * BLOCK:(j + 2) * BLOCK, :]
        rhs = jnp.concatenate([
            jnp.concatenate([
                jnp.concatenate([q_t[(h * GROUP + g) * HEAD_DIM:(h * GROUP + g + 1) * HEAD_DIM,
                                     j * BLOCK:(j + 1) * BLOCK] for g in range(GROUP)], axis=1)
                if hh == h else no_dims for hh in range(N_KV_HEADS)], axis=1)
            for h in range(N_KV_HEADS)], axis=0)
        return jnp.dot(keys, rhs, preferred_element_type=_F32)

    def probs(j, s):
        s = s + bias_ref[...]
        if j == 0:
            s = jnp.where(pad_keys, NEG_INF, s)
        m = jnp.maximum(jnp.max(s, axis=0, keepdims=True), sinks)
        p = jnp.exp(s - m)
        denom = jnp.sum(p, axis=0, keepdims=True) + jnp.exp(sinks - m)
        return p.astype(_BF16), 1.0 / denom

    def weighted_values(j, p, inv):
        r0 = j * BLOCK
        for h in range(N_KV_HEADS):
            lanes = slice(h * GROUP * BLOCK, (h + 1) * GROUP * BLOCK)
            vals = vt_ext[h * HEAD_DIM:(h + 1) * HEAD_DIM, r0:r0 + 2 * BLOCK]
            ot = (jnp.dot(vals, p[:, lanes], preferred_element_type=_F32) * inv[:, lanes]).astype(_BF16)
            for g in range(GROUP):
                d0 = (h * GROUP + g) * HEAD_DIM
                attn_t[d0:d0 + HEAD_DIM, r0:r0 + BLOCK] = ot[:, g * BLOCK:(g + 1) * BLOCK]

    def gate_piece(which, r0, nr, c0, nc):
        col = c_v + which * d_model + c0
        z = jnp.dot(ub[r0:r0 + nr], w_in_ref[:, col:col + nc], preferred_element_type=_F32)
        gates[which, r0:r0 + nr, c0:c0 + nc] = jax.nn.sigmoid(z)

    steps = list(range(n_blocks))
    piece_cols = 2 * LANES
    pieces = [(which, 0, tm, c0, piece_cols) for which in range(2) for c0 in range(0, d_model, piece_cols)]
    lookahead = 2
    pending = [scores(j) for j in steps[:lookahead]]

    rows = tm + POOL_HALO
    pool_hist[POOL_HALO:rows, :] = jnp.dot(ub, w_in_ref[:, 0:c_pool], preferred_element_type=_F32)

    for n, j in enumerate(steps):
        p, inv = probs(j, pending.pop(0))
        if n + lookahead < len(steps):
            pending.append(scores(steps[n + lookahead]))
        for piece in pieces[n * len(pieces) // len(steps):(n + 1) * len(pieces) // len(steps)]:
            gate_piece(*piece)
        weighted_values(j, p, inv)
    k_ext[0:BLOCK, :] = k_ext[tm:tm + BLOCK, :]
    vt_ext[:, 0:BLOCK] = vt_ext[:, tm:tm + BLOCK]

    y_attn = lax.dot_general(attn_t[...], p_attn_ref[...], (((0,), (0,)), ((), ())),
                             preferred_element_type=_F32)

    w_up_o_ref[...] = w_up_ref[...].astype(_BF16)
    w_down_o_ref[...] = w_down_ref[...].astype(_BF16)

    pool_s2[8:rows, :] = pool_hist[8:rows, :] + pool_hist[7:rows - 1, :]
    pool_s4[16:rows, :] = pool_s2[16:rows, 128:] + pool_s2[14:rows - 2, 128:]
    pool_s8[24:rows, :] = pool_s4[24:rows, 128:] + pool_s4[20:rows - 4, 128:]
    s16 = pool_s8[32:rows, 128:] + pool_s8[24:rows - 8, 128:]
    wsum = jnp.concatenate([pool_s2[32:rows, 0:128], pool_s4[32:rows, 0:128],
                            pool_s8[32:rows, 0:128], s16], axis=1)
    t1 = i * tm + lax.broadcasted_iota(jnp.int32, (tm, c_pool), 0) + 1
    win = jnp.left_shift(2, lax.broadcasted_iota(jnp.int32, (tm, c_pool), 1) // POOL_GROUP_DIM)
    count = jnp.minimum(t1, win).astype(_F32)
    d = (wsum / count - pool_hist[POOL_HALO:rows, :]).astype(_BF16)
    pool_hist[0:POOL_HALO, :] = pool_hist[tm:rows, :]

    y = jnp.concatenate([jnp.dot(d[:, c * pair:(c + 1) * pair], pool_bd[c], preferred_element_type=_F32)
                         for c in range(n_groups // 2)], axis=1)
    y = ((y + pool_b_ref[...]) * pool_scale_ref[...]).astype(_BF16)
    y_pool = jnp.dot(y, p_pool_ref[...], preferred_element_type=_F32)
    nr = tm // 4
    for r0 in range(0, tm, nr):
        mixed = gates[0, r0:r0 + nr] * y_pool[r0:r0 + nr] + gates[1, r0:r0 + nr] * y_attn[r0:r0 + nr]
        o_ref[r0:r0 + nr, :] = x_ref[r0:r0 + nr, :] + jnp.dot(
            mixed.astype(_BF16), w_out_ref[...], preferred_element_type=_F32)


def _mlp_kernel(h_ref, g_ref, w_up_ref, w_down_ref, gf_ref, o_ref, *, ff_chunk, final_norm):
    d_ff = w_up_ref.shape[1]
    tm = h_ref.shape[0]
    ub = jnp.concatenate([_rms_norm(h_ref[r:r + BLOCK, :], g_ref[...]).astype(_BF16)
                          for r in range(0, tm, BLOCK)], axis=0)
    def up(c):
        a = jnp.maximum(jnp.dot(ub, w_up_ref[:, c:c + ff_chunk], preferred_element_type=_F32), 0.0)
        return (a * a).astype(_BF16)

    chunks = list(range(0, d_ff, ff_chunk))
    acc = h_ref[...]
    act = up(chunks[0])
    for n, c in enumerate(chunks):
        nxt = up(chunks[n + 1]) if n + 1 < len(chunks) else None
        acc = acc + jnp.dot(act, w_down_ref[c:c + ff_chunk, :], preferred_element_type=_F32)
        act = nxt
    if final_norm:
        acc = _rms_norm(acc, gf_ref[...])
    o_ref[...] = acc


def _resident(shape):
    zeros = (0,) * len(shape)
    return pl.BlockSpec(shape, lambda *_: zeros, pipeline_mode=pl.Buffered(1))


def _mixer(x, sinks, g, w_in, pool_w, pool_b, pool_scale, bias, p_pool, p_attn, w_out, w_up, w_down):
    b, s, d = x.shape
    tm = MIXER_TILE
    assert s % tm == 0 and tm % BLOCK == 0
    n_i = s // tm
    d_ff = w_up.shape[1]
    assert d_ff % (b * n_i * LANES) == 0
    ff_slice = d_ff // (b * n_i)
    c_pool = len(POOL_WINDOWS) * POOL_GROUP_DIM
    attn_width = N_KV_HEADS * GROUP * HEAD_DIM
    grid_spec = pltpu.PrefetchScalarGridSpec(
        num_scalar_prefetch=1,
        grid=(b, n_i),
        in_specs=[
            pl.BlockSpec((None, tm, d), lambda bi, i, _: (bi, i, 0)),
            _resident(g.shape), pl.BlockSpec(memory_space=pl.ANY), _resident(pool_w.shape),
            _resident(pool_b.shape), _resident(pool_scale.shape), _resident(bias.shape),
            pl.BlockSpec(memory_space=pl.ANY), pl.BlockSpec(memory_space=pl.ANY), pl.BlockSpec(memory_space=pl.ANY),
            pl.BlockSpec((d, ff_slice), lambda bi, i, _: (0, bi * n_i + i)),
            pl.BlockSpec((ff_slice, d), lambda bi, i, _: (bi * n_i + i, 0)),
        ],
        out_specs=[
            pl.BlockSpec((None, tm, d), lambda bi, i, _: (bi, i, 0)),
            pl.BlockSpec((d, ff_slice), lambda bi, i, _: (0, bi * n_i + i)),
            pl.BlockSpec((ff_slice, d), lambda bi, i, _: (bi * n_i + i, 0)),
        ],
        scratch_shapes=[
            pltpu.VMEM(w_in.shape, _BF16), pltpu.VMEM(p_pool.shape, _BF16),
            pltpu.VMEM(p_attn.shape, _BF16), pltpu.VMEM(w_out.shape, _BF16),
            pltpu.VMEM((WEIGHT_STAGE_SLOTS,) + WEIGHT_STAGE_BLOCK, _F32),
            pltpu.SemaphoreType.DMA((WEIGHT_STAGE_SLOTS,)),
            pltpu.VMEM((tm + POOL_HALO, c_pool), _F32),
            pltpu.VMEM((tm + POOL_HALO, c_pool), _F32),
            pltpu.VMEM((tm + POOL_HALO, c_pool - POOL_GROUP_DIM), _F32),
            pltpu.VMEM((tm + POOL_HALO, c_pool - 2 * POOL_GROUP_DIM), _F32),
            pltpu.VMEM((len(POOL_WINDOWS) // 2, 2 * POOL_GROUP_DIM, 2 * POOL_GROUP_DIM), _BF16),
            pltpu.VMEM((attn_width, d), _BF16),
            pltpu.VMEM((tm + BLOCK, N_KV_HEADS * HEAD_DIM), _BF16),
            pltpu.VMEM((N_KV_HEADS * HEAD_DIM, tm + BLOCK), _BF16),
            pltpu.VMEM((attn_width, tm), _BF16),
            pltpu.VMEM((2, tm, d), _F32),
            pltpu.VMEM((attn_width, tm), _BF16),
        ],
    )
    return pl.pallas_call(
        functools.partial(_mixer_kernel, tm=tm),
        grid_spec=grid_spec,
        out_shape=[jax.ShapeDtypeStruct(x.shape, _F32),
                   jax.ShapeDtypeStruct(w_up.shape, _BF16), jax.ShapeDtypeStruct(w_down.shape, _BF16)],
        compiler_params=pltpu.CompilerParams(dimension_semantics=("arbitrary", "arbitrary")),
        name="mixer",
    )(sinks, x, g, w_in, pool_w, pool_b, pool_scale, bias, p_pool, p_attn, w_out, w_up, w_down)


def _mlp(h, g, w_up, w_down, g_final, final_norm):
    t, d = h.shape
    tm = MLP_TILE
    assert t % tm == 0
    return pl.pallas_call(
        functools.partial(_mlp_kernel, ff_chunk=1024, final_norm=final_norm),
        grid=(t // tm,),
        in_specs=[
            pl.BlockSpec((tm, d), lambda i: (i, 0)),
            _resident(g.shape), _resident(w_up.shape), _resident(w_down.shape), _resident(g_final.shape),
        ],
        out_specs=pl.BlockSpec((tm, d), lambda i: (i, 0)),
        out_shape=jax.ShapeDtypeStruct(h.shape, _F32),
        compiler_params=pltpu.CompilerParams(dimension_semantics=("arbitrary",)),
        name="mlp",
    )(h, g, w_up, w_down, g_final)


def kernel(x, norm_mix, w_in, pool_w, pool_b, pool_scale, attn_sinks, p_pool, p_attn,
           w_out, norm_mlp, w_up, w_down, norm_final):
    depth = norm_mix.shape[0]
    b, s, d = x.shape
    bias = jnp.asarray(_attention_bias())
    h = x
    for l in range(depth):
        h, w_up_b, w_down_b = _mixer(
            h, attn_sinks[l], norm_mix[l][None, :], w_in[l], pool_w[l], pool_b[l].reshape(1, -1),
            pool_scale[l][None, :], bias, p_pool[l], p_attn[l], w_out[l], w_up[l], w_down[l])
        h = _mlp(h.reshape(b * s, d), norm_mlp[l][None, :], w_up_b, w_down_b,
                 norm_final[None, :], l == depth - 1).reshape(b, s, d)
    return h
```

```python
import functools

import jax
import jax.numpy as jnp
import numpy as np
from jax import lax
from jax.experimental import pallas as pl
from jax.experimental.pallas import tpu as pltpu

HEAD_DIM = 64
N_KV_HEADS = 2
GROUP = 4
BLOCK = 128
POOL_WINDOWS = (2, 4, 8, 16)
POOL_GROUP_DIM = 128
POOL_HALO = 32
RMS_EPS = 1e-5
NEG_INF = -1e30
LANES = 128

MIXER_TILE = 1024
WEIGHT_STAGE_BLOCK = (256, 1024)
WEIGHT_STAGE_SLOTS = 4
MLP_TILE = 1024

_F32 = jnp.float32
_BF16 = jnp.bfloat16


def _rms_norm(x, g):
    return x * lax.rsqrt(jnp.mean(x * x, axis=-1, keepdims=True) + RMS_EPS) * g


def _attention_bias():
    kj = np.arange(2 * BLOCK)[:, None]
    qi = np.arange(BLOCK)[None, :]
    dist = (BLOCK + qi - kj).astype(np.float32)
    valid = (dist >= 0) & (dist < BLOCK)
    n_heads = N_KV_HEADS * GROUP
    out = np.empty((2 * BLOCK, n_heads * BLOCK), np.float32)
    for head in range(n_heads):
        slope = np.float32(2.0 ** (-8.0 * (head + 1) / n_heads))
        out[:, head * BLOCK:(head + 1) * BLOCK] = np.where(valid, -(slope * dist), np.float32(NEG_INF))
    return out


def _mixer_kernel(sinks_ref, x_ref, g_ref, w_in_hbm, pool_w_ref, pool_b_ref, pool_scale_ref,
                  bias_ref, p_pool_hbm, p_attn_hbm, w_out_hbm, w_up_ref, w_down_ref,
                  o_ref, w_up_o_ref, w_down_o_ref,
                  w_in_ref, p_pool_ref, p_attn_ref, w_out_ref, stage, stage_sem,
                  pool_hist, pool_s2, pool_s4, pool_s8, pool_bd, wq_t, k_ext, vt_ext, q_t, gates, attn_t, *, tm):
    i = pl.program_id(1)
    n_groups = len(POOL_WINDOWS)
    c_pool = n_groups * POOL_GROUP_DIM
    c_q = c_pool + N_KV_HEADS * GROUP * HEAD_DIM
    c_v = c_q + 2 * N_KV_HEADS * HEAD_DIM
    d_model = x_ref.shape[-1]
    n_blocks = tm // BLOCK
    pair = 2 * POOL_GROUP_DIM

    @pl.when(i == 0)
    def _():
        pool_hist[0:POOL_HALO, :] = jnp.zeros((POOL_HALO, c_pool), _F32)
        k_ext[0:BLOCK, :] = jnp.zeros((BLOCK, LANES), _BF16)
        vt_ext[:, 0:BLOCK] = jnp.zeros((LANES, BLOCK), _BF16)

    @pl.when(jnp.logical_and(pl.program_id(0) == 0, i == 0))
    def _():
        srows, scols = stage.shape[1:]
        assert c_q <= scols
        chunks = []
        for src, dst in ((w_in_hbm, w_in_ref), (p_pool_hbm, p_pool_ref), (p_attn_hbm, p_attn_ref),
                         (w_out_hbm, w_out_ref)):
            for r in range(0, src.shape[0], srows):
                for c in range(0, src.shape[1], scols):
                    w = min(scols, src.shape[1] - c)
                    chunks.append((src.at[r:r + srows, c:c + w], dst.at[r:r + srows, c:c + w], r, w,
                                   src is w_in_hbm and c == 0))

        def chunk_copy(n):
            slot = n % stage.shape[0]
            return pltpu.make_async_copy(chunks[n][0], stage.at[slot, :, 0:chunks[n][3]], stage_sem.at[slot])

        for n in range(min(stage.shape[0], len(chunks))):
            chunk_copy(n).start()
        for n, (_, dst, r, w, has_q) in enumerate(chunks):
            chunk_copy(n).wait()
            vals = stage[n % stage.shape[0], :, 0:w]
            dst[...] = vals.astype(_BF16)
            if has_q:
                wq_t[:, r:r + srows] = vals[:, c_pool:c_q].T.astype(_BF16)
            if n + stage.shape[0] < len(chunks):
                chunk_copy(n + stage.shape[0]).start()
        pool_bd[...] = jnp.zeros(pool_bd.shape, _BF16)
        for g in range(n_groups):
            lo = (g % 2) * POOL_GROUP_DIM
            pool_bd[g // 2, lo:lo + POOL_GROUP_DIM, lo:lo + POOL_GROUP_DIM] = pool_w_ref[g].astype(_BF16)

    ub = jnp.concatenate([_rms_norm(x_ref[r:r + BLOCK, :], g_ref[...]).astype(_BF16)
                          for r in range(0, tm, BLOCK)], axis=0)

    kv = jnp.dot(ub, w_in_ref[:, c_q:c_v], preferred_element_type=_F32)
    q_t[...] = (lax.dot_general(wq_t[...], ub, (((1,), (1,)), ((), ())), preferred_element_type=_F32)
                * (1.0 / float(np.sqrt(HEAD_DIM)))).astype(_BF16)
    k_ext[BLOCK:BLOCK + tm, :] = kv[:, 0:LANES].astype(_BF16)
    vt_ext[:, BLOCK:BLOCK + tm] = kv[:, LANES:2 * LANES].T.astype(_BF16)

    n_heads = N_KV_HEADS * GROUP
    krow = lax.broadcasted_iota(jnp.int32, (2 * BLOCK, n_heads * BLOCK), 0)
    pad_keys = jnp.logical_and(krow < BLOCK, i == 0)
    sinks = jnp.concatenate([jnp.full((1, BLOCK), sinks_ref[head], _F32) for head in range(n_heads)], axis=1)
    no_dims = jnp.zeros((HEAD_DIM, GROUP * BLOCK), _BF16)

    def scores(j):
        keys = k_ext[j * BLOCK:(j + 2) * BLOCK, :]
        rhs = jnp.concatenate([
            jnp.concatenate([
                jnp.concatenate([q_t[(h * GROUP + g) * HEAD_DIM:(h * GROUP + g + 1) * HEAD_DIM,
                                     j * BLOCK:(j + 1) * BLOCK] for g in range(GROUP)], axis=1)
                if hh == h else no_dims for hh in range(N_KV_HEADS)], axis=1)
            for h in range(N_KV_HEADS)], axis=0)
        return jnp.dot(keys, rhs, preferred_element_type=_F32)

    def probs(j, s):
        s = s + bias_ref[...]
        if j == 0:
            s = jnp.where(pad_keys, NEG_INF, s)
        m = jnp.maximum(jnp.max(s, axis=0, keepdims=True), sinks)
        p = jnp.exp(s - m)
        denom = jnp.sum(p, axis=0, keepdims=True) + jnp.exp(sinks - m)
        return p.astype(_BF16), 1.0 / denom

    def weighted_values(j, p, inv):
        r0 = j * BLOCK
        for h in range(N_KV_HEADS):
            lanes = slice(h * GROUP * BLOCK, (h + 1) * GROUP * BLOCK)
            vals = vt_ext[h * HEAD_DIM:(h + 1) * HEAD_DIM, r0:r0 + 2 * BLOCK]
            ot = (jnp.dot(vals, p[:, lanes], preferred_element_type=_F32) * inv[:, lanes]).astype(_BF16)
            for g in range(GROUP):
                d0 = (h * GROUP + g) * HEAD_DIM
                attn_t[d0:d0 + HEAD_DIM, r0:r0 + BLOCK] = ot[:, g * BLOCK:(g + 1) * BLOCK]

    def gate_piece(which, r0, nr, c0, nc):
        col = c_v + which * d_model + c0
        z = jnp.dot(ub[r0:r0 + nr], w_in_ref[:, col:col + nc], preferred_element_type=_F32)
        gates[which, r0:r0 + nr, c0:c0 + nc] = jax.nn.sigmoid(z)

    steps = list(range(n_blocks))
    piece_cols = 2 * LANES
    pieces = [(which, 0, tm, c0, piece_cols) for which in range(2) for c0 in range(0, d_model, piece_cols)]
    lookahead = 2
    pending = [scores(j) for j in steps[:lookahead]]

    rows = tm + POOL_HALO
    pool_hist[POOL_HALO:rows, :] = jnp.dot(ub, w_in_ref[:, 0:c_pool], preferred_element_type=_F32)

    for n, j in enumerate(steps):
        p, inv = probs(j, pending.pop(0))
        if n + lookahead < len(steps):
            pending.append(scores(steps[n + lookahead]))
        for piece in pieces[n * len(pieces) // len(steps):(n + 1) * len(pieces) // len(steps)]:
            gate_piece(*piece)
        weighted_values(j, p, inv)
    k_ext[0:BLOCK, :] = k_ext[tm:tm + BLOCK, :]
    vt_ext[:, 0:BLOCK] = vt_ext[:, tm:tm + BLOCK]

    y_attn = lax.dot_general(attn_t[...], p_attn_ref[...], (((0,), (0,)), ((), ())),
                             preferred_element_type=_F32)

    w_up_o_ref[...] = w_up_ref[...].astype(_BF16)
    w_down_o_ref[...] = w_down_ref[...].astype(_BF16)

    pool_s2[8:rows, :] = pool_hist[8:rows, :] + pool_hist[7:rows - 1, :]
    pool_s4[16:rows, :] = pool_s2[16:rows, 128:] + pool_s2[14:rows - 2, 128:]
    pool_s8[24:rows, :] = pool_s4[24:rows, 128:] + pool_s4[20:rows - 4, 128:]
    s16 = pool_s8[32:rows, 128:] + pool_s8[24:rows - 8, 128:]
    wsum = jnp.concatenate([pool_s2[32:rows, 0:128], pool_s4[32:rows, 0:128],
                            pool_s8[32:rows, 0:128], s16], axis=1)
    t1 = i * tm + lax.broadcasted_iota(jnp.int32, (tm, c_pool), 0) + 1
    win = jnp.left_shift(2, lax.broadcasted_iota(jnp.int32, (tm, c_pool), 1) // POOL_GROUP_DIM)
    count = jnp.minimum(t1, win).astype(_F32)
    d = (wsum / count - pool_hist[POOL_HALO:rows, :]).astype(_BF16)
    pool_hist[0:POOL_HALO, :] = pool_hist[tm:rows, :]

    y = jnp.concatenate([jnp.dot(d[:, c * pair:(c + 1) * pair], pool_bd[c], preferred_element_type=_F32)
                         for c in range(n_groups // 2)], axis=1)
    y = ((y + pool_b_ref[...]) * pool_scale_ref[...]).astype(_BF16)
    y_pool = jnp.dot(y, p_pool_ref[...], preferred_element_type=_F32)
    nr = tm // 4
    for r0 in range(0, tm, nr):
        mixed = gates[0, r0:r0 + nr] * y_pool[r0:r0 + nr] + gates[1, r0:r0 + nr] * y_attn[r0:r0 + nr]
        o_ref[r0:r0 + nr, :] = x_ref[r0:r0 + nr, :] + jnp.dot(
            mixed.astype(_BF16), w_out_ref[...], preferred_element_type=_F32)


def _mlp_kernel(h_ref, g_ref, w_up_ref, w_down_ref, gf_ref, o_ref, *, ff_chunk, final_norm):
    d_ff = w_up_ref.shape[1]
    tm = h_ref.shape[0]
    ub = jnp.concatenate([_rms_norm(h_ref[r:r + BLOCK, :], g_ref[...]).astype(_BF16)
                          for r in range(0, tm, BLOCK)], axis=0)
    def up(c):
        a = jnp.maximum(jnp.dot(ub, w_up_ref[:, c:c + ff_chunk], preferred_element_type=_F32), 0.0)
        return (a * a).astype(_BF16)

    chunks = list(range(0, d_ff, ff_chunk))
    acc = h_ref[...]
    act = up(chunks[0])
    for n, c in enumerate(chunks):
        nxt = up(chunks[n + 1]) if n + 1 < len(chunks) else None
        acc = acc + jnp.dot(act, w_down_ref[c:c + ff_chunk, :], preferred_element_type=_F32)
        act = nxt
    if final_norm:
        acc = _rms_norm(acc, gf_ref[...])
    o_ref[...] = acc


def _resident(shape):
    zeros = (0,) * len(shape)
    return pl.BlockSpec(shape, lambda *_: zeros, pipeline_mode=pl.Buffered(1))


def _mixer(x, sinks, g, w_in, pool_w, pool_b, pool_scale, bias, p_pool, p_attn, w_out, w_up, w_down):
    b, s, d = x.shape
    tm = MIXER_TILE
    assert s % tm == 0 and tm % BLOCK == 0
    n_i = s // tm
    d_ff = w_up.shape[1]
    assert d_ff % (b * n_i * LANES) == 0
    ff_slice = d_ff // (b * n_i)
    c_pool = len(POOL_WINDOWS) * POOL_GROUP_DIM
    attn_width = N_KV_HEADS * GROUP * HEAD_DIM
    grid_spec = pltpu.PrefetchScalarGridSpec(
        num_scalar_prefetch=1,
        grid=(b, n_i),
        in_specs=[
            pl.BlockSpec((None, tm, d), lambda bi, i, _: (bi, i, 0)),
            _resident(g.shape), pl.BlockSpec(memory_space=pl.ANY), _resident(pool_w.shape),
            _resident(pool_b.shape), _resident(pool_scale.shape), _resident(bias.shape),
            pl.BlockSpec(memory_space=pl.ANY), pl.BlockSpec(memory_space=pl.ANY), pl.BlockSpec(memory_space=pl.ANY),
            pl.BlockSpec((d, ff_slice), lambda bi, i, _: (0, bi * n_i + i)),
            pl.BlockSpec((ff_slice, d), lambda bi, i, _: (bi * n_i + i, 0)),
        ],
        out_specs=[
            pl.BlockSpec((None, tm, d), lambda bi, i, _: (bi, i, 0)),
            pl.BlockSpec((d, ff_slice), lambda bi, i, _: (0, bi * n_i + i)),
            pl.BlockSpec((ff_slice, d), lambda bi, i, _: (bi * n_i + i, 0)),
        ],
        scratch_shapes=[
            pltpu.VMEM(w_in.shape, _BF16), pltpu.VMEM(p_pool.shape, _BF16),
            pltpu.VMEM(p_attn.shape, _BF16), pltpu.VMEM(w_out.shape, _BF16),
            pltpu.VMEM((WEIGHT_STAGE_SLOTS,) + WEIGHT_STAGE_BLOCK, _F32),
            pltpu.SemaphoreType.DMA((WEIGHT_STAGE_SLOTS,)),
            pltpu.VMEM((tm + POOL_HALO, c_pool), _F32),
            pltpu.VMEM((tm + POOL_HALO, c_pool), _F32),
            pltpu.VMEM((tm + POOL_HALO, c_pool - POOL_GROUP_DIM), _F32),
            pltpu.VMEM((tm + POOL_HALO, c_pool - 2 * POOL_GROUP_DIM), _F32),
            pltpu.VMEM((len(POOL_WINDOWS) // 2, 2 * POOL_GROUP_DIM, 2 * POOL_GROUP_DIM), _BF16),
            pltpu.VMEM((attn_width, d), _BF16),
            pltpu.VMEM((tm + BLOCK, N_KV_HEADS * HEAD_DIM), _BF16),
            pltpu.VMEM((N_KV_HEADS * HEAD_DIM, tm + BLOCK), _BF16),
            pltpu.VMEM((attn_width, tm), _BF16),
            pltpu.VMEM((2, tm, d), _F32),
            pltpu.VMEM((attn_width, tm), _BF16),
        ],
    )
    return pl.pallas_call(
        functools.partial(_mixer_kernel, tm=tm),
        grid_spec=grid_spec,
        out_shape=[jax.ShapeDtypeStruct(x.shape, _F32),
                   jax.ShapeDtypeStruct(w_up.shape, _BF16), jax.ShapeDtypeStruct(w_down.shape, _BF16)],
        compiler_params=pltpu.CompilerParams(dimension_semantics=("arbitrary", "arbitrary")),
        name="mixer",
    )(sinks, x, g, w_in, pool_w, pool_b, pool_scale, bias, p_pool, p_attn, w_out, w_up, w_down)


def _mlp(h, g, w_up, w_down, g_final, final_norm):
    t, d = h.shape
    tm = MLP_TILE
    assert t % tm == 0
    return pl.pallas_call(
        functools.partial(_mlp_kernel, ff_chunk=2048, final_norm=final_norm),
        grid=(t // tm,),
        in_specs=[
            pl.BlockSpec((tm, d), lambda i: (i, 0)),
            _resident(g.shape), _resident(w_up.shape), _resident(w_down.shape), _resident(g_final.shape),
        ],
        out_specs=pl.BlockSpec((tm, d), lambda i: (i, 0)),
        out_shape=jax.ShapeDtypeStruct(h.shape, _F32),
        compiler_params=pltpu.CompilerParams(dimension_semantics=("arbitrary",)),
        name="mlp",
    )(h, g, w_up, w_down, g_final)


def kernel(x, norm_mix, w_in, pool_w, pool_b, pool_scale, attn_sinks, p_pool, p_attn,
           w_out, norm_mlp, w_up, w_down, norm_final):
    depth = norm_mix.shape[0]
    b, s, d = x.shape
    bias = jnp.asarray(_attention_bias())
    h = x
    for l in range(depth):
        h, w_up_b, w_down_b = _mixer(
            h, attn_sinks[l], norm_mix[l][None, :], w_in[l], pool_w[l], pool_b[l].reshape(1, -1),
            pool_scale[l][None, :], bias, p_pool[l], p_attn[l], w_out[l], w_up[l], w_down[l])
        h = _mlp(h.reshape(b * s, d), norm_mlp[l][None, :], w_up_b, w_down_b,
                 norm_final[None, :], l == depth - 1).reshape(b, s, d)
    return h
```

```python
import functools

import jax
import jax.numpy as jnp
import numpy as np
from jax import lax
from jax.experimental import pallas as pl
from jax.experimental.pallas import tpu as pltpu

HEAD_DIM = 64
N_KV_HEADS = 2
GROUP = 4
BLOCK = 128
POOL_WINDOWS = (2, 4, 8, 16)
POOL_GROUP_DIM = 128
POOL_HALO = 32
RMS_EPS = 1e-5
NEG_INF = -1e30
LANES = 128

MIXER_TILE = 1024
WEIGHT_STAGE_BLOCK = (256, 1024)
WEIGHT_STAGE_SLOTS = 4
MLP_TILE = 1024

_F32 = jnp.float32
_BF16 = jnp.bfloat16


def _rms_norm(x, g):
    return x * lax.rsqrt(jnp.mean(x * x, axis=-1, keepdims=True) + RMS_EPS) * g


def _attention_bias():
    kj = np.arange(2 * BLOCK)[:, None]
    qi = np.arange(BLOCK)[None, :]
    dist = (BLOCK + qi - kj).astype(np.float32)
    valid = (dist >= 0) & (dist < BLOCK)
    n_heads = N_KV_HEADS * GROUP
    out = np.empty((2 * BLOCK, n_heads * BLOCK), np.float32)
    for head in range(n_heads):
        slope = np.float32(2.0 ** (-8.0 * (head + 1) / n_heads))
        out[:, head * BLOCK:(head + 1) * BLOCK] = np.where(valid, -(slope * dist), np.float32(NEG_INF))
    return out


def _mixer_kernel(sinks_ref, x_ref, g_ref, w_in_hbm, pool_w_ref, pool_b_ref, pool_scale_ref,
                  bias_ref, p_pool_hbm, p_attn_hbm, w_out_hbm, w_up_ref, w_down_ref,
                  o_ref, w_up_o_ref, w_down_o_ref,
                  w_in_ref, p_pool_ref, p_attn_ref, w_out_ref, stage, stage_sem,
                  pool_hist, pool_s2, pool_s4, pool_s8, pool_bd, wq_t, k_ext, vt_ext, q_t, gates, attn_t, *, tm):
    i = pl.program_id(1)
    n_groups = len(POOL_WINDOWS)
    c_pool = n_groups * POOL_GROUP_DIM
    c_q = c_pool + N_KV_HEADS * GROUP * HEAD_DIM
    c_v = c_q + 2 * N_KV_HEADS * HEAD_DIM
    d_model = x_ref.shape[-1]
    n_blocks = tm // BLOCK
    pair = 2 * POOL_GROUP_DIM

    @pl.when(i == 0)
    def _():
        pool_hist[0:POOL_HALO, :] = jnp.zeros((POOL_HALO, c_pool), _F32)
        k_ext[0:BLOCK, :] = jnp.zeros((BLOCK, LANES), _BF16)
        vt_ext[:, 0:BLOCK] = jnp.zeros((LANES, BLOCK), _BF16)

    @pl.when(jnp.logical_and(pl.program_id(0) == 0, i == 0))
    def _():
        srows, scols = stage.shape[1:]
        assert c_q == scols
        chunks = []
        for src, dst in ((w_in_hbm, w_in_ref), (p_pool_hbm, p_pool_ref), (p_attn_hbm, p_attn_ref),
                         (w_out_hbm, w_out_ref)):
            for r in range(0, src.shape[0], srows):
                for c in range(0, src.shape[1], scols):
                    w = min(scols, src.shape[1] - c)
                    has_q = src is w_in_hbm and c == 0
                    keep = c_pool if has_q else w
                    c_dst = c - (c_q - c_pool) if src is w_in_hbm and c > 0 else c
                    chunks.append((src.at[r:r + srows, c:c + w], dst.at[r:r + srows, c_dst:c_dst + keep], r, w,
                                   keep, has_q))

        def chunk_copy(n):
            slot = n % stage.shape[0]
            return pltpu.make_async_copy(chunks[n][0], stage.at[slot, :, 0:chunks[n][3]], stage_sem.at[slot])

        for n in range(min(stage.shape[0], len(chunks))):
            chunk_copy(n).start()
        for n, (_, dst, r, w, keep, has_q) in enumerate(chunks):
            chunk_copy(n).wait()
            vals = stage[n % stage.shape[0], :, 0:w]
            dst[...] = vals[:, 0:keep].astype(_BF16)
            if has_q:
                wq_t[:, r:r + srows] = vals[:, c_pool:c_q].T.astype(_BF16)
            if n + stage.shape[0] < len(chunks):
                chunk_copy(n + stage.shape[0]).start()
        pool_bd[...] = jnp.zeros(pool_bd.shape, _BF16)
        for g in range(n_groups):
            lo = (g % 2) * POOL_GROUP_DIM
            pool_bd[g // 2, lo:lo + POOL_GROUP_DIM, lo:lo + POOL_GROUP_DIM] = pool_w_ref[g].astype(_BF16)

    ub = jnp.concatenate([_rms_norm(x_ref[r:r + BLOCK, :], g_ref[...]).astype(_BF16)
                          for r in range(0, tm, BLOCK)], axis=0)

    c_kv = c_pool + c_v - c_q
    kv = jnp.dot(ub, w_in_ref[:, c_pool:c_kv], preferred_element_type=_F32)
    q_t[...] = (lax.dot_general(wq_t[...], ub, (((1,), (1,)), ((), ())), preferred_element_type=_F32)
                * (1.0 / float(np.sqrt(HEAD_DIM)))).astype(_BF16)
    k_ext[BLOCK:BLOCK + tm, :] = kv[:, 0:LANES].astype(_BF16)
    vt_ext[:, BLOCK:BLOCK + tm] = kv[:, LANES:2 * LANES].T.astype(_BF16)

    n_heads = N_KV_HEADS * GROUP
    krow = lax.broadcasted_iota(jnp.int32, (2 * BLOCK, n_heads * BLOCK), 0)
    pad_keys = jnp.logical_and(krow < BLOCK, i == 0)
    sinks = jnp.concatenate([jnp.full((1, BLOCK), sinks_ref[head], _F32) for head in range(n_heads)], axis=1)
    no_dims = jnp.zeros((HEAD_DIM, GROUP * BLOCK), _BF16)

    def scores(j):
        keys = k_ext[j * BLOCK:(j + 2) * BLOCK, :]
        rhs = jnp.concatenate([
            jnp.concatenate([
                jnp.concatenate([q_t[(h * GROUP + g) * HEAD_DIM:(h * GROUP + g + 1) * HEAD_DIM,
                                     j * BLOCK:(j + 1) * BLOCK] for g in range(GROUP)], axis=1)
                if hh == h else no_dims for hh in range(N_KV_HEADS)], axis=1)
            for h in range(N_KV_HEADS)], axis=0)
        return jnp.dot(keys, rhs, preferred_element_type=_F32)

    def probs(j, s):
        s = s + bias_ref[...]
        if j == 0:
            s = jnp.where(pad_keys, NEG_INF, s)
        m = jnp.maximum(jnp.max(s, axis=0, keepdims=True), sinks)
        p = jnp.exp(s - m)
        denom = jnp.sum(p, axis=0, keepdims=True) + jnp.exp(sinks - m)
        return p.astype(_BF16), 1.0 / denom

    def weighted_values(j, p, inv):
        r0 = j * BLOCK
        ot_all = jnp.dot(vt_ext[:, r0:r0 + 2 * BLOCK], p, preferred_element_type=_F32)
        for h in range(N_KV_HEADS):
            lanes = slice(h * GROUP * BLOCK, (h + 1) * GROUP * BLOCK)
            ot = (ot_all[h * HEAD_DIM:(h + 1) * HEAD_DIM, lanes] * inv[:, lanes]).astype(_BF16)
            for g in range(GROUP):
                d0 = (h * GROUP + g) * HEAD_DIM
                attn_t[d0:d0 + HEAD_DIM, r0:r0 + BLOCK] = ot[:, g * BLOCK:(g + 1) * BLOCK]

    def gate_piece(which, r0, nr, c0, nc):
        col = c_kv + which * d_model + c0
        z = jnp.dot(ub[r0:r0 + nr], w_in_ref[:, col:col + nc], preferred_element_type=_F32)
        gates[which, r0:r0 + nr, c0:c0 + nc] = jax.nn.sigmoid(z)

    steps = list(range(n_blocks))
    piece_cols = 2 * LANES
    pieces = [(which, 0, tm, c0, piece_cols) for which in range(2) for c0 in range(0, d_model, piece_cols)]
    lookahead = 2
    pending = [scores(j) for j in steps[:lookahead]]

    rows = tm + POOL_HALO
    pool_hist[POOL_HALO:rows, :] = jnp.dot(ub, w_in_ref[:, 0:c_pool], preferred_element_type=_F32)

    for n, j in enumerate(steps):
        p, inv = probs(j, pending.pop(0))
        if n + lookahead < len(steps):
            pending.append(scores(steps[n + lookahead]))
        for piece in pieces[n * len(pieces) // len(steps):(n + 1) * len(pieces) // len(steps)]:
            gate_piece(*piece)
        weighted_values(j, p, inv)
    k_ext[0:BLOCK, :] = k_ext[tm:tm + BLOCK, :]
    vt_ext[:, 0:BLOCK] = vt_ext[:, tm:tm + BLOCK]

    y_attn = lax.dot_general(attn_t[...], p_attn_ref[...], (((0,), (0,)), ((), ())),
                             preferred_element_type=_F32)

    w_up_o_ref[...] = w_up_ref[...].astype(_BF16)
    w_down_o_ref[...] = w_down_ref[...].astype(_BF16)

    pool_s2[8:rows, :] = pool_hist[8:rows, :] + pool_hist[7:rows - 1, :]
    pool_s4[16:rows, :] = pool_s2[16:rows, 128:] + pool_s2[14:rows - 2, 128:]
    pool_s8[24:rows, :] = pool_s4[24:rows, 128:] + pool_s4[20:rows - 4, 128:]
    s16 = pool_s8[32:rows, 128:] + pool_s8[24:rows - 8, 128:]
    wsum = jnp.concatenate([pool_s2[32:rows, 0:128], pool_s4[32:rows, 0:128],
                            pool_s8[32:rows, 0:128], s16], axis=1)
    t1 = i * tm + lax.broadcasted_iota(jnp.int32, (tm, c_pool), 0) + 1
    win = jnp.left_shift(2, lax.broadcasted_iota(jnp.int32, (tm, c_pool), 1) // POOL_GROUP_DIM)
    count = jnp.minimum(t1, win).astype(_F32)
    d = (wsum / count - pool_hist[POOL_HALO:rows, :]).astype(_BF16)
    pool_hist[0:POOL_HALO, :] = pool_hist[tm:rows, :]

    y = jnp.concatenate([jnp.dot(d[:, c * pair:(c + 1) * pair], pool_bd[c], preferred_element_type=_F32)
                         for c in range(n_groups // 2)], axis=1)
    y = ((y + pool_b_ref[...]) * pool_scale_ref[...]).astype(_BF16)
    y_pool = jnp.dot(y, p_pool_ref[...], preferred_element_type=_F32)
    nr = tm // 4
    for r0 in range(0, tm, nr):
        mixed = gates[0, r0:r0 + nr] * y_pool[r0:r0 + nr] + gates[1, r0:r0 + nr] * y_attn[r0:r0 + nr]
        o_ref[r0:r0 + nr, :] = x_ref[r0:r0 + nr, :] + jnp.dot(
            mixed.astype(_BF16), w_out_ref[...], preferred_element_type=_F32)


def _mlp_kernel(h_ref, g_ref, w_up_ref, w_down_ref, gf_ref, o_ref, *, ff_chunk, final_norm):
    d_ff = w_up_ref.shape[1]
    tm = h_ref.shape[0]
    ub = jnp.concatenate([_rms_norm(h_ref[r:r + BLOCK, :], g_ref[...]).astype(_BF16)
                          for r in range(0, tm, BLOCK)], axis=0)
    def up(c):
        a = jnp.maximum(jnp.dot(ub, w_up_ref[:, c:c + ff_chunk], preferred_element_type=_F32), 0.0)
        return (a * a).astype(_BF16)

    chunks = list(range(0, d_ff, ff_chunk))
    acc = h_ref[...]
    act = up(chunks[0])
    for n, c in enumerate(chunks):
        nxt = up(chunks[n + 1]) if n + 1 < len(chunks) else None
        acc = acc + jnp.dot(act, w_down_ref[c:c + ff_chunk, :], preferred_element_type=_F32)
        act = nxt
    if final_norm:
        acc = _rms_norm(acc, gf_ref[...])
    o_ref[...] = acc


def _resident(shape):
    zeros = (0,) * len(shape)
    return pl.BlockSpec(shape, lambda *_: zeros, pipeline_mode=pl.Buffered(1))


def _mixer(x, sinks, g, w_in, pool_w, pool_b, pool_scale, bias, p_pool, p_attn, w_out, w_up, w_down):
    b, s, d = x.shape
    tm = MIXER_TILE
    assert s % tm == 0 and tm % BLOCK == 0
    n_i = s // tm
    d_ff = w_up.shape[1]
    assert d_ff % (b * n_i * LANES) == 0
    ff_slice = d_ff // (b * n_i)
    c_pool = len(POOL_WINDOWS) * POOL_GROUP_DIM
    attn_width = N_KV_HEADS * GROUP * HEAD_DIM
    grid_spec = pltpu.PrefetchScalarGridSpec(
        num_scalar_prefetch=1,
        grid=(b, n_i),
        in_specs=[
            pl.BlockSpec((None, tm, d), lambda bi, i, _: (bi, i, 0)),
            _resident(g.shape), pl.BlockSpec(memory_space=pl.ANY), _resident(pool_w.shape),
            _resident(pool_b.shape), _resident(pool_scale.shape), _resident(bias.shape),
            pl.BlockSpec(memory_space=pl.ANY), pl.BlockSpec(memory_space=pl.ANY), pl.BlockSpec(memory_space=pl.ANY),
            pl.BlockSpec((d, ff_slice), lambda bi, i, _: (0, bi * n_i + i)),
            pl.BlockSpec((ff_slice, d), lambda bi, i, _: (bi * n_i + i, 0)),
        ],
        out_specs=[
            pl.BlockSpec((None, tm, d), lambda bi, i, _: (bi, i, 0)),
            pl.BlockSpec((d, ff_slice), lambda bi, i, _: (0, bi * n_i + i)),
            pl.BlockSpec((ff_slice, d), lambda bi, i, _: (bi * n_i + i, 0)),
        ],
        scratch_shapes=[
            pltpu.VMEM((d, w_in.shape[1] - attn_width), _BF16), pltpu.VMEM(p_pool.shape, _BF16),
            pltpu.VMEM(p_attn.shape, _BF16), pltpu.VMEM(w_out.shape, _BF16),
            pltpu.VMEM((WEIGHT_STAGE_SLOTS,) + WEIGHT_STAGE_BLOCK, _F32),
            pltpu.SemaphoreType.DMA((WEIGHT_STAGE_SLOTS,)),
            pltpu.VMEM((tm + POOL_HALO, c_pool), _F32),
            pltpu.VMEM((tm + POOL_HALO, c_pool), _F32),
            pltpu.VMEM((tm + POOL_HALO, c_pool - POOL_GROUP_DIM), _F32),
            pltpu.VMEM((tm + POOL_HALO, c_pool - 2 * POOL_GROUP_DIM), _F32),
            pltpu.VMEM((len(POOL_WINDOWS) // 2, 2 * POOL_GROUP_DIM, 2 * POOL_GROUP_DIM), _BF16),
            pltpu.VMEM((attn_width, d), _BF16),
            pltpu.VMEM((tm + BLOCK, N_KV_HEADS * HEAD_DIM), _BF16),
            pltpu.VMEM((N_KV_HEADS * HEAD_DIM, tm + BLOCK), _BF16),
            pltpu.VMEM((attn_width, tm), _BF16),
            pltpu.VMEM((2, tm, d), _F32),
            pltpu.VMEM((attn_width, tm), _BF16),
        ],
    )
    return pl.pallas_call(
        functools.partial(_mixer_kernel, tm=tm),
        grid_spec=grid_spec,
        out_shape=[jax.ShapeDtypeStruct(x.shape, _F32),
                   jax.ShapeDtypeStruct(w_up.shape, _BF16), jax.ShapeDtypeStruct(w_down.shape, _BF16)],
        compiler_params=pltpu.CompilerParams(dimension_semantics=("arbitrary", "arbitrary")),
        name="mixer",
    )(sinks, x, g, w_in, pool_w, pool_b, pool_scale, bias, p_pool, p_attn, w_out, w_up, w_down)


def _mlp(h, g, w_up, w_down, g_final, final_norm):
    t, d = h.shape
    tm = MLP_TILE
    assert t % tm == 0
    return pl.pallas_call(
        functools.partial(_mlp_kernel, ff_chunk=1024, final_norm=final_norm),
        grid=(t // tm,),
        in_specs=[
            pl.BlockSpec((tm, d), lambda i: (i, 0)),
            _resident(g.shape), _resident(w_up.shape), _resident(w_down.shape), _resident(g_final.shape),
        ],
        out_specs=pl.BlockSpec((tm, d), lambda i: (i, 0)),
        out_shape=jax.ShapeDtypeStruct(h.shape, _F32),
        compiler_params=pltpu.CompilerParams(dimension_semantics=("arbitrary",)),
        name="mlp",
    )(h, g, w_up, w_down, g_final)


def kernel(x, norm_mix, w_in, pool_w, pool_b, pool_scale, attn_sinks, p_pool, p_attn,
           w_out, norm_mlp, w_up, w_down, norm_final):
    depth = norm_mix.shape[0]
    b, s, d = x.shape
    bias = jnp.asarray(_attention_bias())
    h = x
    for l in range(depth):
        h, w_up_b, w_down_b = _mixer(
            h, attn_sinks[l], norm_mix[l][None, :], w_in[l], pool_w[l], pool_b[l].reshape(1, -1),
            pool_scale[l][None, :], bias, p_pool[l], p_attn[l], w_out[l], w_up[l], w_down[l])
        h = _mlp(h.reshape(b * s, d), norm_mlp[l][None, :], w_up_b, w_down_b,
                 norm_final[None, :], l == depth - 1).reshape(b, s, d)
    return h
```

```python
import functools

import jax
import jax.numpy as jnp
import numpy as np
from jax import lax
from jax.experimental import pallas as pl
from jax.experimental.pallas import tpu as pltpu

HEAD_DIM = 64
N_KV_HEADS = 2
GROUP = 4
BLOCK = 128
POOL_WINDOWS = (2, 4, 8, 16)
POOL_GROUP_DIM = 128
POOL_HALO = 32
RMS_EPS = 1e-5
NEG_INF = -1e30
LANES = 128

MIXER_TILE = 1024
WEIGHT_STAGE_BLOCK = (256, 1024)
WEIGHT_STAGE_SLOTS = 4
MLP_TILE = 1024

_F32 = jnp.float32
_BF16 = jnp.bfloat16


def _rms_norm(x, g):
    return x * lax.rsqrt(jnp.mean(x * x, axis=-1, keepdims=True) + RMS_EPS) * g


def _attention_bias():
    kj = np.arange(2 * BLOCK)[:, None]
    qi = np.arange(BLOCK)[None, :]
    dist = (BLOCK + qi - kj).astype(np.float32)
    valid = (dist >= 0) & (dist < BLOCK)
    n_heads = N_KV_HEADS * GROUP
    out = np.empty((2 * BLOCK, n_heads * BLOCK), np.float32)
    for head in range(n_heads):
        slope = np.float32(2.0 ** (-8.0 * (head + 1) / n_heads))
        out[:, head * BLOCK:(head + 1) * BLOCK] = np.where(valid, -(slope * dist), np.float32(NEG_INF))
    return out


def _mixer_kernel(sinks_ref, x_ref, g_ref, w_in_hbm, pool_w_ref, pool_b_ref, pool_scale_ref,
                  bias_ref, p_pool_hbm, p_attn_hbm, w_out_hbm, w_up_ref, w_down_ref,
                  o_ref, w_up_o_ref, w_down_o_ref,
                  w_in_ref, p_pool_ref, p_attn_ref, w_out_ref, stage, stage_sem,
                  pool_hist, pool_s2, pool_s4, pool_s8, pool_bd, wq_t, k_ext, vt_ext, q_t, gates, attn_t, *, tm):
    i = pl.program_id(1)
    n_groups = len(POOL_WINDOWS)
    c_pool = n_groups * POOL_GROUP_DIM
    c_q = c_pool + N_KV_HEADS * GROUP * HEAD_DIM
    c_v = c_q + 2 * N_KV_HEADS * HEAD_DIM
    d_model = x_ref.shape[-1]
    n_blocks = tm // BLOCK
    pair = 2 * POOL_GROUP_DIM

    @pl.when(i == 0)
    def _():
        pool_hist[0:POOL_HALO, :] = jnp.zeros((POOL_HALO, c_pool), _F32)
        k_ext[0:BLOCK, :] = jnp.zeros((BLOCK, LANES), _BF16)
        vt_ext[:, 0:BLOCK] = jnp.zeros((LANES, BLOCK), _BF16)

    @pl.when(jnp.logical_and(pl.program_id(0) == 0, i == 0))
    def _():
        srows, scols = stage.shape[1:]
        assert c_q == scols
        chunks = []
        for src, dst in ((w_in_hbm, w_in_ref), (p_pool_hbm, p_pool_ref), (p_attn_hbm, p_attn_ref),
                         (w_out_hbm, w_out_ref)):
            for r in range(0, src.shape[0], srows):
                for c in range(0, src.shape[1], scols):
                    w = min(scols, src.shape[1] - c)
                    has_q = src is w_in_hbm and c == 0
                    keep = c_pool if has_q else w
                    c_dst = c - (c_q - c_pool) if src is w_in_hbm and c > 0 else c
                    chunks.append((src.at[r:r + srows, c:c + w], dst.at[r:r + srows, c_dst:c_dst + keep], r, w,
                                   keep, has_q))

        def chunk_copy(n):
            slot = n % stage.shape[0]
            return pltpu.make_async_copy(chunks[n][0], stage.at[slot, :, 0:chunks[n][3]], stage_sem.at[slot])

        for n in range(min(stage.shape[0], len(chunks))):
            chunk_copy(n).start()
        for n, (_, dst, r, w, keep, has_q) in enumerate(chunks):
            chunk_copy(n).wait()
            vals = stage[n % stage.shape[0], :, 0:w]
            dst[...] = vals[:, 0:keep].astype(_BF16)
            if has_q:
                wq_t[:, r:r + srows] = vals[:, c_pool:c_q].T.astype(_BF16)
            if n + stage.shape[0] < len(chunks):
                chunk_copy(n + stage.shape[0]).start()
        pool_bd[...] = jnp.zeros(pool_bd.shape, _BF16)
        for g in range(n_groups):
            lo = (g % 2) * POOL_GROUP_DIM
            pool_bd[g // 2, lo:lo + POOL_GROUP_DIM, lo:lo + POOL_GROUP_DIM] = pool_w_ref[g].astype(_BF16)

    ub = jnp.concatenate([_rms_norm(x_ref[r:r + BLOCK, :], g_ref[...]).astype(_BF16)
                          for r in range(0, tm, BLOCK)], axis=0)

    c_kv = c_pool + c_v - c_q
    kv = jnp.dot(ub, w_in_ref[:, c_pool:c_kv], preferred_element_type=_F32)
    q_t[...] = (lax.dot_general(wq_t[...], ub, (((1,), (1,)), ((), ())), preferred_element_type=_F32)
                * (1.0 / float(np.sqrt(HEAD_DIM)))).astype(_BF16)
    k_ext[BLOCK:BLOCK + tm, :] = kv[:, 0:LANES].astype(_BF16)
    vt_ext[:, BLOCK:BLOCK + tm] = kv[:, LANES:2 * LANES].T.astype(_BF16)

    n_heads = N_KV_HEADS * GROUP
    krow = lax.broadcasted_iota(jnp.int32, (2 * BLOCK, n_heads * BLOCK), 0)
    pad_keys = jnp.logical_and(krow < BLOCK, i == 0)
    sinks = jnp.concatenate([jnp.full((1, BLOCK), sinks_ref[head], _F32) for head in range(n_heads)], axis=1)
    no_dims = jnp.zeros((HEAD_DIM, GROUP * BLOCK), _BF16)

    def scores(j):
        keys = k_ext[j * BLOCK:(j + 2) * BLOCK, :]
        rhs = jnp.concatenate([
            jnp.concatenate([
                jnp.concatenate([q_t[(h * GROUP + g) * HEAD_DIM:(h * GROUP + g + 1) * HEAD_DIM,
                                     j * BLOCK:(j + 1) * BLOCK] for g in range(GROUP)], axis=1)
                if hh == h else no_dims for hh in range(N_KV_HEADS)], axis=1)
            for h in range(N_KV_HEADS)], axis=0)
        return jnp.dot(keys, rhs, preferred_element_type=_F32)

    def probs(j, s):
        s = s + bias_ref[...]
        if j == 0:
            s = jnp.where(pad_keys, NEG_INF, s)
        m = jnp.maximum(jnp.max(s, axis=0, keepdims=True), sinks)
        p = jnp.exp(s - m)
        denom = jnp.sum(p, axis=0, keepdims=True) + jnp.exp(sinks - m)
        return p.astype(_BF16), 1.0 / denom

    def weighted_values(j, p, inv):
        r0 = j * BLOCK
        for h in range(N_KV_HEADS):
            lanes = slice(h * GROUP * BLOCK, (h + 1) * GROUP * BLOCK)
            vals = vt_ext[h * HEAD_DIM:(h + 1) * HEAD_DIM, r0:r0 + 2 * BLOCK]
            ot = (jnp.dot(vals, p[:, lanes], preferred_element_type=_F32) * inv[:, lanes]).astype(_BF16)
            for g in range(GROUP):
                d0 = (h * GROUP + g) * HEAD_DIM
                attn_t[d0:d0 + HEAD_DIM, r0:r0 + BLOCK] = ot[:, g * BLOCK:(g + 1) * BLOCK]

    def gate_piece(which, r0, nr, c0, nc):
        col = c_kv + which * d_model + c0
        z = jnp.dot(ub[r0:r0 + nr], w_in_ref[:, col:col + nc], preferred_element_type=_F32)
        gates[which, r0:r0 + nr, c0:c0 + nc] = jax.nn.sigmoid(z)

    steps = list(range(n_blocks))
    piece_cols = 2 * LANES
    pieces = [(which, 0, tm, c0, piece_cols) for which in range(2) for c0 in range(0, d_model, piece_cols)]
    lookahead = 2
    pending = [scores(j) for j in steps[:lookahead]]

    rows = tm + POOL_HALO
    pool_hist[POOL_HALO:rows, :] = jnp.dot(ub, w_in_ref[:, 0:c_pool], preferred_element_type=_F32)

    for n, j in enumerate(steps):
        p, inv = probs(j, pending.pop(0))
        if n + lookahead < len(steps):
            pending.append(scores(steps[n + lookahead]))
        for piece in pieces[n * len(pieces) // len(steps):(n + 1) * len(pieces) // len(steps)]:
            gate_piece(*piece)
        weighted_values(j, p, inv)
    k_ext[0:BLOCK, :] = k_ext[tm:tm + BLOCK, :]
    vt_ext[:, 0:BLOCK] = vt_ext[:, tm:tm + BLOCK]

    y_attn = lax.dot_general(attn_t[...], p_attn_ref[...], (((0,), (0,)), ((), ())),
                             preferred_element_type=_F32)

    w_up_o_ref[...] = w_up_ref[...].astype(_BF16)
    w_down_o_ref[...] = w_down_ref[...].astype(_BF16)

    pool_s2[8:rows, :] = pool_hist[8:rows, :] + pool_hist[7:rows - 1, :]
    pool_s4[16:rows, :] = pool_s2[16:rows, 128:] + pool_s2[14:rows - 2, 128:]
    pool_s8[24:rows, :] = pool_s4[24:rows, 128:] + pool_s4[20:rows - 4, 128:]
    s16 = pool_s8[32:rows, 128:] + pool_s8[24:rows - 8, 128:]
    wsum = jnp.concatenate([pool_s2[32:rows, 0:128], pool_s4[32:rows, 0:128],
                            pool_s8[32:rows, 0:128], s16], axis=1)
    t1 = i * tm + lax.broadcasted_iota(jnp.int32, (tm, c_pool), 0) + 1
    win = jnp.left_shift(2, lax.broadcasted_iota(jnp.int32, (tm, c_pool), 1) // POOL_GROUP_DIM)
    count = jnp.minimum(t1, win).astype(_F32)
    d = (wsum / count - pool_hist[POOL_HALO:rows, :]).astype(_BF16)
    pool_hist[0:POOL_HALO, :] = pool_hist[tm:rows, :]

    y = jnp.concatenate([jnp.dot(d[:, c * pair:(c + 1) * pair], pool_bd[c], preferred_element_type=_F32)
                         for c in range(n_groups // 2)], axis=1)
    y = ((y + pool_b_ref[...]) * pool_scale_ref[...]).astype(_BF16)
    y_pool = jnp.dot(y, p_pool_ref[...], preferred_element_type=_F32)
    nr = tm // 4
    for r0 in range(0, tm, nr):
        mixed = gates[0, r0:r0 + nr] * y_pool[r0:r0 + nr] + gates[1, r0:r0 + nr] * y_attn[r0:r0 + nr]
        o_ref[r0:r0 + nr, :] = x_ref[r0:r0 + nr, :] + jnp.dot(
            mixed.astype(_BF16), w_out_ref[...], preferred_element_type=_F32)


def _mlp_kernel(h_ref, g_ref, w_up_ref, w_down_ref, gf_ref, o_ref, *, ff_chunk, final_norm):
    d_ff = w_up_ref.shape[1]
    tm = h_ref.shape[0]
    ub = jnp.concatenate([_rms_norm(h_ref[r:r + BLOCK, :], g_ref[...]).astype(_BF16)
                          for r in range(0, tm, BLOCK)], axis=0)
    def up(c):
        a = jnp.maximum(jnp.dot(ub, w_up_ref[:, c:c + ff_chunk], preferred_element_type=_F32), 0.0)
        return (a * a).astype(_BF16)

    chunks = list(range(0, d_ff, ff_chunk))
    acc = h_ref[...]
    act = up(chunks[0])
    for n, c in enumerate(chunks[:-1]):
        nxt = up(chunks[n + 1])
        acc = acc + jnp.dot(act, w_down_ref[c:c + ff_chunk, :], preferred_element_type=_F32)
        act = nxt
    nr = tm // 4
    for r0 in range(0, tm, nr):
        out = acc[r0:r0 + nr] + jnp.dot(act[r0:r0 + nr], w_down_ref[chunks[-1]:chunks[-1] + ff_chunk, :],
                                        preferred_element_type=_F32)
        if final_norm:
            out = _rms_norm(out, gf_ref[...])
        o_ref[r0:r0 + nr, :] = out


def _resident(shape):
    zeros = (0,) * len(shape)
    return pl.BlockSpec(shape, lambda *_: zeros, pipeline_mode=pl.Buffered(1))


def _mixer(x, sinks, g, w_in, pool_w, pool_b, pool_scale, bias, p_pool, p_attn, w_out, w_up, w_down):
    b, s, d = x.shape
    tm = MIXER_TILE
    assert s % tm == 0 and tm % BLOCK == 0
    n_i = s // tm
    d_ff = w_up.shape[1]
    assert d_ff % (b * n_i * LANES) == 0
    ff_slice = d_ff // (b * n_i)
    c_pool = len(POOL_WINDOWS) * POOL_GROUP_DIM
    attn_width = N_KV_HEADS * GROUP * HEAD_DIM
    grid_spec = pltpu.PrefetchScalarGridSpec(
        num_scalar_prefetch=1,
        grid=(b, n_i),
        in_specs=[
            pl.BlockSpec((None, tm, d), lambda bi, i, _: (bi, i, 0)),
            _resident(g.shape), pl.BlockSpec(memory_space=pl.ANY), _resident(pool_w.shape),
            _resident(pool_b.shape), _resident(pool_scale.shape), _resident(bias.shape),
            pl.BlockSpec(memory_space=pl.ANY), pl.BlockSpec(memory_space=pl.ANY), pl.BlockSpec(memory_space=pl.ANY),
            pl.BlockSpec((d, ff_slice), lambda bi, i, _: (0, bi * n_i + i)),
            pl.BlockSpec((ff_slice, d), lambda bi, i, _: (bi * n_i + i, 0)),
        ],
        out_specs=[
            pl.BlockSpec((None, tm, d), lambda bi, i, _: (bi, i, 0)),
            pl.BlockSpec((d, ff_slice), lambda bi, i, _: (0, bi * n_i + i)),
            pl.BlockSpec((ff_slice, d), lambda bi, i, _: (bi * n_i + i, 0)),
        ],
        scratch_shapes=[
            pltpu.VMEM((d, w_in.shape[1] - attn_width), _BF16), pltpu.VMEM(p_pool.shape, _BF16),
            pltpu.VMEM(p_attn.shape, _BF16), pltpu.VMEM(w_out.shape, _BF16),
            pltpu.VMEM((WEIGHT_STAGE_SLOTS,) + WEIGHT_STAGE_BLOCK, _F32),
            pltpu.SemaphoreType.DMA((WEIGHT_STAGE_SLOTS,)),
            pltpu.VMEM((tm + POOL_HALO, c_pool), _F32),
            pltpu.VMEM((tm + POOL_HALO, c_pool), _F32),
            pltpu.VMEM((tm + POOL_HALO, c_pool - POOL_GROUP_DIM), _F32),
            pltpu.VMEM((tm + POOL_HALO, c_pool - 2 * POOL_GROUP_DIM), _F32),
            pltpu.VMEM((len(POOL_WINDOWS) // 2, 2 * POOL_GROUP_DIM, 2 * POOL_GROUP_DIM), _BF16),
            pltpu.VMEM((attn_width, d), _BF16),
            pltpu.VMEM((tm + BLOCK, N_KV_HEADS * HEAD_DIM), _BF16),
            pltpu.VMEM((N_KV_HEADS * HEAD_DIM, tm + BLOCK), _BF16),
            pltpu.VMEM((attn_width, tm), _BF16),
            pltpu.VMEM((2, tm, d), _F32),
            pltpu.VMEM((attn_width, tm), _BF16),
        ],
    )
    return pl.pallas_call(
        functools.partial(_mixer_kernel, tm=tm),
        grid_spec=grid_spec,
        out_shape=[jax.ShapeDtypeStruct(x.shape, _F32),
                   jax.ShapeDtypeStruct(w_up.shape, _BF16), jax.ShapeDtypeStruct(w_down.shape, _BF16)],
        compiler_params=pltpu.CompilerParams(dimension_semantics=("arbitrary", "arbitrary")),
        name="mixer",
    )(sinks, x, g, w_in, pool_w, pool_b, pool_scale, bias, p_pool, p_attn, w_out, w_up, w_down)


def _mlp(h, g, w_up, w_down, g_final, final_norm):
    t, d = h.shape
    tm = MLP_TILE
    assert t % tm == 0
    return pl.pallas_call(
        functools.partial(_mlp_kernel, ff_chunk=1024, final_norm=final_norm),
        grid=(t // tm,),
        in_specs=[
            pl.BlockSpec((tm, d), lambda i: (i, 0)),
            _resident(g.shape), _resident(w_up.shape), _resident(w_down.shape), _resident(g_final.shape),
        ],
        out_specs=pl.BlockSpec((tm, d), lambda i: (i, 0)),
        out_shape=jax.ShapeDtypeStruct(h.shape, _F32),
        compiler_params=pltpu.CompilerParams(dimension_semantics=("arbitrary",)),
        name="mlp",
    )(h, g, w_up, w_down, g_final)


def kernel(x, norm_mix, w_in, pool_w, pool_b, pool_scale, attn_sinks, p_pool, p_attn,
           w_out, norm_mlp, w_up, w_down, norm_final):
    depth = norm_mix.shape[0]
    b, s, d = x.shape
    bias = jnp.asarray(_attention_bias())
    h = x
    for l in range(depth):
        h, w_up_b, w_down_b = _mixer(
            h, attn_sinks[l], norm_mix[l][None, :], w_in[l], pool_w[l], pool_b[l].reshape(1, -1),
            pool_scale[l][None, :], bias, p_pool[l], p_attn[l], w_out[l], w_up[l], w_down[l])
        h = _mlp(h.reshape(b * s, d), norm_mlp[l][None, :], w_up_b, w_down_b,
                 norm_final[None, :], l == depth - 1).reshape(b, s, d)
    return h
```

```python
import functools

import jax
import jax.numpy as jnp
import numpy as np
from jax import lax
from jax.experimental import pallas as pl
from jax.experimental.pallas import tpu as pltpu

HEAD_DIM = 64
N_KV_HEADS = 2
GROUP = 4
BLOCK = 128
POOL_WINDOWS = (2, 4, 8, 16)
POOL_GROUP_DIM = 128
POOL_HALO = 32
RMS_EPS = 1e-5
NEG_INF = -1e30
LANES = 128

MIXER_TILE = 1024
WEIGHT_STAGE_BLOCK = (256, 1024)
WEIGHT_STAGE_SLOTS = 4
MLP_TILE = 1024
MIXER_VMEM_BYTES = 60000 * 1024
MLP_VMEM_BYTES = 48 * 1024 * 1024

_F32 = jnp.float32
_BF16 = jnp.bfloat16


def _rms_norm(x, g):
    return x * lax.rsqrt(jnp.mean(x * x, axis=-1, keepdims=True) + RMS_EPS) * g


def _attention_bias():
    kj = np.arange(2 * BLOCK)[:, None]
    qi = np.arange(BLOCK)[None, :]
    dist = (BLOCK + qi - kj).astype(np.float32)
    valid = (dist >= 0) & (dist < BLOCK)
    n_heads = N_KV_HEADS * GROUP
    out = np.empty((2 * BLOCK, n_heads * BLOCK), np.float32)
    for head in range(n_heads):
        slope = np.float32(2.0 ** (-8.0 * (head + 1) / n_heads))
        out[:, head * BLOCK:(head + 1) * BLOCK] = np.where(valid, -(slope * dist), np.float32(NEG_INF))
    return out


def _mixer_kernel(sinks_ref, x_ref, g_ref, w_in_hbm, pool_w_ref, pool_b_ref, pool_scale_ref,
                  bias_ref, p_pool_hbm, p_attn_hbm, w_out_hbm, w_up_ref, w_down_ref,
                  o_ref, w_up_o_ref, w_down_o_ref,
                  w_in_ref, p_pool_ref, p_attn_ref, w_out_ref, stage, stage_sem,
                  pool_hist, pool_s2, pool_s4, pool_s8, pool_bd, wq_t, k_ext, vt_ext, q_t, gates, attn_t, *, tm):
    i = pl.program_id(1)
    n_groups = len(POOL_WINDOWS)
    c_pool = n_groups * POOL_GROUP_DIM
    c_q = c_pool + N_KV_HEADS * GROUP * HEAD_DIM
    c_v = c_q + 2 * N_KV_HEADS * HEAD_DIM
    d_model = x_ref.shape[-1]
    n_blocks = tm // BLOCK
    pair = 2 * POOL_GROUP_DIM

    @pl.when(i == 0)
    def _():
        pool_hist[0:POOL_HALO, :] = jnp.zeros((POOL_HALO, c_pool), _F32)
        k_ext[0:BLOCK, :] = jnp.zeros((BLOCK, LANES), _BF16)
        vt_ext[:, 0:BLOCK] = jnp.zeros((LANES, BLOCK), _BF16)

    @pl.when(jnp.logical_and(pl.program_id(0) == 0, i == 0))
    def _():
        srows, scols = stage.shape[1:]
        assert c_q == scols
        chunks = []
        for src, dst in ((w_in_hbm, w_in_ref), (p_pool_hbm, p_pool_ref), (p_attn_hbm, p_attn_ref),
                         (w_out_hbm, w_out_ref)):
            for r in range(0, src.shape[0], srows):
                for c in range(0, src.shape[1], scols):
                    w = min(scols, src.shape[1] - c)
                    has_q = src is w_in_hbm and c == 0
                    keep = c_pool if has_q else w
                    c_dst = c - (c_q - c_pool) if src is w_in_hbm and c > 0 else c
                    chunks.append((src.at[r:r + srows, c:c + w], dst.at[r:r + srows, c_dst:c_dst + keep], r, w,
                                   keep, has_q))

        def chunk_copy(n):
            slot = n % stage.shape[0]
            return pltpu.make_async_copy(chunks[n][0], stage.at[slot, :, 0:chunks[n][3]], stage_sem.at[slot])

        for n in range(min(stage.shape[0], len(chunks))):
            chunk_copy(n).start()
        for n, (_, dst, r, w, keep, has_q) in enumerate(chunks):
            chunk_copy(n).wait()
            vals = stage[n % stage.shape[0], :, 0:w]
            dst[...] = vals[:, 0:keep].astype(_BF16)
            if has_q:
                wq_t[:, r:r + srows] = vals[:, c_pool:c_q].T.astype(_BF16)
            if n + stage.shape[0] < len(chunks):
                chunk_copy(n + stage.shape[0]).start()
        pool_bd[...] = jnp.zeros(pool_bd.shape, _BF16)
        for g in range(n_groups):
            lo = (g % 2) * POOL_GROUP_DIM
            pool_bd[g // 2, lo:lo + POOL_GROUP_DIM, lo:lo + POOL_GROUP_DIM] = pool_w_ref[g].astype(_BF16)

    ub = jnp.concatenate([_rms_norm(x_ref[r:r + BLOCK, :], g_ref[...]).astype(_BF16)
                          for r in range(0, tm, BLOCK)], axis=0)

    c_kv = c_pool + c_v - c_q
    kv = jnp.dot(ub, w_in_ref[:, c_pool:c_kv], preferred_element_type=_F32)
    q_t[...] = (lax.dot_general(wq_t[...], ub, (((1,), (1,)), ((), ())), preferred_element_type=_F32)
                * (1.0 / float(np.sqrt(HEAD_DIM)))).astype(_BF16)
    k_ext[BLOCK:BLOCK + tm, :] = kv[:, 0:LANES].astype(_BF16)
    vt_ext[:, BLOCK:BLOCK + tm] = kv[:, LANES:2 * LANES].T.astype(_BF16)

    n_heads = N_KV_HEADS * GROUP
    krow = lax.broadcasted_iota(jnp.int32, (2 * BLOCK, n_heads * BLOCK), 0)
    pad_keys = jnp.logical_and(krow < BLOCK, i == 0)
    sinks = jnp.concatenate([jnp.full((1, BLOCK), sinks_ref[head], _F32) for head in range(n_heads)], axis=1)
    no_dims = jnp.zeros((HEAD_DIM, GROUP * BLOCK), _BF16)

    def scores(j):
        keys = k_ext[j * BLOCK:(j + 2) * BLOCK, :]
        rhs = jnp.concatenate([
            jnp.concatenate([
                jnp.concatenate([q_t[(h * GROUP + g) * HEAD_DIM:(h * GROUP + g + 1) * HEAD_DIM,
                                     j * BLOCK:(j + 1) * BLOCK] for g in range(GROUP)], axis=1)
                if hh == h else no_dims for hh in range(N_KV_HEADS)], axis=1)
            for h in range(N_KV_HEADS)], axis=0)
        return jnp.dot(keys, rhs, preferred_element_type=_F32)

    def probs(j, s):
        s = s + bias_ref[...]
        if j == 0:
            s = jnp.where(pad_keys, NEG_INF, s)
        m = jnp.maximum(jnp.max(s, axis=0, keepdims=True), sinks)
        p = jnp.exp(s - m)
        denom = jnp.sum(p, axis=0, keepdims=True) + jnp.exp(sinks - m)
        return p.astype(_BF16), 1.0 / denom

    def weighted_values(j, p, inv):
        r0 = j * BLOCK
        for h in range(N_KV_HEADS):
            lanes = slice(h * GROUP * BLOCK, (h + 1) * GROUP * BLOCK)
            vals = vt_ext[h * HEAD_DIM:(h + 1) * HEAD_DIM, r0:r0 + 2 * BLOCK]
            ot = (jnp.dot(vals, p[:, lanes], preferred_element_type=_F32) * inv[:, lanes]).astype(_BF16)
            for g in range(GROUP):
                d0 = (h * GROUP + g) * HEAD_DIM
                attn_t[d0:d0 + HEAD_DIM, r0:r0 + BLOCK] = ot[:, g * BLOCK:(g + 1) * BLOCK]

    def gate_piece(which, r0, nr, c0, nc):
        col = c_kv + which * d_model + c0
        z = jnp.dot(ub[r0:r0 + nr], w_in_ref[:, col:col + nc], preferred_element_type=_F32)
        gates[which, r0:r0 + nr, c0:c0 + nc] = jax.nn.sigmoid(z)

    steps = list(range(n_blocks))
    piece_cols = 2 * LANES
    pieces = [(which, 0, tm, c0, piece_cols) for which in range(2) for c0 in range(0, d_model, piece_cols)]
    lookahead = 2
    pending = [scores(j) for j in steps[:lookahead]]

    rows = tm + POOL_HALO
    pool_hist[POOL_HALO:rows, :] = jnp.dot(ub, w_in_ref[:, 0:c_pool], preferred_element_type=_F32)

    for n, j in enumerate(steps):
        p, inv = probs(j, pending.pop(0))
        if n + lookahead < len(steps):
            pending.append(scores(steps[n + lookahead]))
        for piece in pieces[n * len(pieces) // len(steps):(n + 1) * len(pieces) // len(steps)]:
            gate_piece(*piece)
        weighted_values(j, p, inv)
    k_ext[0:BLOCK, :] = k_ext[tm:tm + BLOCK, :]
    vt_ext[:, 0:BLOCK] = vt_ext[:, tm:tm + BLOCK]

    y_attn = lax.dot_general(attn_t[...], p_attn_ref[...], (((0,), (0,)), ((), ())),
                             preferred_element_type=_F32)

    w_up_o_ref[...] = w_up_ref[...].astype(_BF16)
    w_down_o_ref[...] = w_down_ref[...].astype(_BF16)

    assert POOL_WINDOWS == tuple(2 ** (k + 1) for k in range(n_groups)) and POOL_HALO == 8 * n_groups
    src, lane0, group_sums = pool_hist, 0, []
    for k, dst in enumerate((pool_s2, pool_s4, pool_s8, None)):
        first, shift = 8 * (k + 1), 2 ** k
        window = src[first:rows, lane0:] + src[first - shift:rows - shift, lane0:]
        if dst is None:
            group_sums.append(window)
        else:
            dst[first:rows, :] = window
            group_sums.append(dst[POOL_HALO:rows, 0:POOL_GROUP_DIM])
            src, lane0 = dst, POOL_GROUP_DIM
    wsum = jnp.concatenate(group_sums, axis=1)
    t1 = i * tm + lax.broadcasted_iota(jnp.int32, (tm, c_pool), 0) + 1
    win = jnp.left_shift(2, lax.broadcasted_iota(jnp.int32, (tm, c_pool), 1) // POOL_GROUP_DIM)
    count = jnp.minimum(t1, win).astype(_F32)
    d = (wsum / count - pool_hist[POOL_HALO:rows, :]).astype(_BF16)
    pool_hist[0:POOL_HALO, :] = pool_hist[tm:rows, :]

    y = jnp.concatenate([jnp.dot(d[:, c * pair:(c + 1) * pair], pool_bd[c], preferred_element_type=_F32)
                         for c in range(n_groups // 2)], axis=1)
    y = ((y + pool_b_ref[...]) * pool_scale_ref[...]).astype(_BF16)
    y_pool = jnp.dot(y, p_pool_ref[...], preferred_element_type=_F32)
    nr = tm // 4
    for r0 in range(0, tm, nr):
        mixed = gates[0, r0:r0 + nr] * y_pool[r0:r0 + nr] + gates[1, r0:r0 + nr] * y_attn[r0:r0 + nr]
        o_ref[r0:r0 + nr, :] = x_ref[r0:r0 + nr, :] + jnp.dot(
            mixed.astype(_BF16), w_out_ref[...], preferred_element_type=_F32)


def _mlp_kernel(h_ref, g_ref, w_up_ref, w_down_ref, gf_ref, o_ref, *, ff_chunk, final_norm):
    d_ff = w_up_ref.shape[1]
    tm = h_ref.shape[0]
    ub = jnp.concatenate([_rms_norm(h_ref[r:r + BLOCK, :], g_ref[...]).astype(_BF16)
                          for r in range(0, tm, BLOCK)], axis=0)
    def up(c):
        a = jnp.maximum(jnp.dot(ub, w_up_ref[:, c:c + ff_chunk], preferred_element_type=_F32), 0.0)
        return (a * a).astype(_BF16)

    chunks = list(range(0, d_ff, ff_chunk))
    acc = h_ref[...]
    act = up(chunks[0])
    for n, c in enumerate(chunks[:-1]):
        nxt = up(chunks[n + 1])
        acc = acc + jnp.dot(act, w_down_ref[c:c + ff_chunk, :], preferred_element_type=_F32)
        act = nxt
    nr = tm // 4
    for r0 in range(0, tm, nr):
        out = acc[r0:r0 + nr] + jnp.dot(act[r0:r0 + nr], w_down_ref[chunks[-1]:chunks[-1] + ff_chunk, :],
                                        preferred_element_type=_F32)
        if final_norm:
            out = _rms_norm(out, gf_ref[...])
        o_ref[r0:r0 + nr, :] = out


def _resident(shape):
    zeros = (0,) * len(shape)
    return pl.BlockSpec(shape, lambda *_: zeros, pipeline_mode=pl.Buffered(1))


def _mixer(x, sinks, g, w_in, pool_w, pool_b, pool_scale, bias, p_pool, p_attn, w_out, w_up, w_down):
    b, s, d = x.shape
    tm = MIXER_TILE
    assert s % tm == 0 and tm % BLOCK == 0
    n_i = s // tm
    d_ff = w_up.shape[1]
    assert d_ff % (b * n_i * LANES) == 0
    ff_slice = d_ff // (b * n_i)
    c_pool = len(POOL_WINDOWS) * POOL_GROUP_DIM
    attn_width = N_KV_HEADS * GROUP * HEAD_DIM
    grid_spec = pltpu.PrefetchScalarGridSpec(
        num_scalar_prefetch=1,
        grid=(b, n_i),
        in_specs=[
            pl.BlockSpec((None, tm, d), lambda bi, i, _: (bi, i, 0)),
            _resident(g.shape), pl.BlockSpec(memory_space=pl.ANY), _resident(pool_w.shape),
            _resident(pool_b.shape), _resident(pool_scale.shape), _resident(bias.shape),
            pl.BlockSpec(memory_space=pl.ANY), pl.BlockSpec(memory_space=pl.ANY), pl.BlockSpec(memory_space=pl.ANY),
            pl.BlockSpec((d, ff_slice), lambda bi, i, _: (0, bi * n_i + i)),
            pl.BlockSpec((ff_slice, d), lambda bi, i, _: (bi * n_i + i, 0)),
        ],
        out_specs=[
            pl.BlockSpec((None, tm, d), lambda bi, i, _: (bi, i, 0)),
            pl.BlockSpec((d, ff_slice), lambda bi, i, _: (0, bi * n_i + i)),
            pl.BlockSpec((ff_slice, d), lambda bi, i, _: (bi * n_i + i, 0)),
        ],
        scratch_shapes=[
            pltpu.VMEM((d, w_in.shape[1] - attn_width), _BF16), pltpu.VMEM(p_pool.shape, _BF16),
            pltpu.VMEM(p_attn.shape, _BF16), pltpu.VMEM(w_out.shape, _BF16),
            pltpu.VMEM((WEIGHT_STAGE_SLOTS,) + WEIGHT_STAGE_BLOCK, _F32),
            pltpu.SemaphoreType.DMA((WEIGHT_STAGE_SLOTS,)),
            pltpu.VMEM((tm + POOL_HALO, c_pool), _F32),
            pltpu.VMEM((tm + POOL_HALO, c_pool), _F32),
            pltpu.VMEM((tm + POOL_HALO, c_pool - POOL_GROUP_DIM), _F32),
            pltpu.VMEM((tm + POOL_HALO, c_pool - 2 * POOL_GROUP_DIM), _F32),
            pltpu.VMEM((len(POOL_WINDOWS) // 2, 2 * POOL_GROUP_DIM, 2 * POOL_GROUP_DIM), _BF16),
            pltpu.VMEM((attn_width, d), _BF16),
            pltpu.VMEM((tm + BLOCK, N_KV_HEADS * HEAD_DIM), _BF16),
            pltpu.VMEM((N_KV_HEADS * HEAD_DIM, tm + BLOCK), _BF16),
            pltpu.VMEM((attn_width, tm), _BF16),
            pltpu.VMEM((2, tm, d), _F32),
            pltpu.VMEM((attn_width, tm), _BF16),
        ],
    )
    return pl.pallas_call(
        functools.partial(_mixer_kernel, tm=tm),
        grid_spec=grid_spec,
        out_shape=[jax.ShapeDtypeStruct(x.shape, _F32),
                   jax.ShapeDtypeStruct(w_up.shape, _BF16), jax.ShapeDtypeStruct(w_down.shape, _BF16)],
        compiler_params=pltpu.CompilerParams(dimension_semantics=("arbitrary", "arbitrary"),
                                             vmem_limit_bytes=MIXER_VMEM_BYTES),
        name="mixer",
    )(sinks, x, g, w_in, pool_w, pool_b, pool_scale, bias, p_pool, p_attn, w_out, w_up, w_down)


def _mlp(h, g, w_up, w_down, g_final, final_norm):
    t, d = h.shape
    tm = MLP_TILE
    assert t % tm == 0
    return pl.pallas_call(
        functools.partial(_mlp_kernel, ff_chunk=1024, final_norm=final_norm),
        grid=(t // tm,),
        in_specs=[
            pl.BlockSpec((tm, d), lambda i: (i, 0)),
            _resident(g.shape), _resident(w_up.shape), _resident(w_down.shape), _resident(g_final.shape),
        ],
        out_specs=pl.BlockSpec((tm, d), lambda i: (i, 0)),
        out_shape=jax.ShapeDtypeStruct(h.shape, _F32),
        compiler_params=pltpu.CompilerParams(dimension_semantics=("arbitrary",), vmem_limit_bytes=MLP_VMEM_BYTES),
        name="mlp",
    )(h, g, w_up, w_down, g_final)


def kernel(x, norm_mix, w_in, pool_w, pool_b, pool_scale, attn_sinks, p_pool, p_attn,
           w_out, norm_mlp, w_up, w_down, norm_final):
    depth = norm_mix.shape[0]
    b, s, d = x.shape
    bias = jnp.asarray(_attention_bias())
    h = x
    for l in range(depth):
        h, w_up_b, w_down_b = _mixer(
            h, attn_sinks[l], norm_mix[l][None, :], w_in[l], pool_w[l], pool_b[l].reshape(1, -1),
            pool_scale[l][None, :], bias, p_pool[l], p_attn[l], w_out[l], w_up[l], w_down[l])
        h = _mlp(h.reshape(b * s, d), norm_mlp[l][None, :], w_up_b, w_down_b,
                 norm_final[None, :], l == depth - 1).reshape(b, s, d)
    return h
```

```python
import functools

import jax
import jax.numpy as jnp
import numpy as np
from jax import lax
from jax.experimental import pallas as pl
from jax.experimental.pallas import tpu as pltpu

HEAD_DIM = 64
N_KV_HEADS = 2
GROUP = 4
BLOCK = 128
POOL_WINDOWS = (2, 4, 8, 16)
POOL_GROUP_DIM = 128
POOL_HALO = 32
RMS_EPS = 1e-5
NEG_INF = -1e30
LANES = 128

MIXER_TILE = 1024
WEIGHT_STAGE_BLOCK = (256, 1024)
WEIGHT_STAGE_SLOTS = 4
MLP_TILE = 1024
MIXER_VMEM_BYTES = 60000 * 1024
MLP_VMEM_BYTES = 48 * 1024 * 1024

_F32 = jnp.float32
_BF16 = jnp.bfloat16


def _rms_norm(x, g):
    return x * lax.rsqrt(jnp.mean(x * x, axis=-1, keepdims=True) + RMS_EPS) * g


def _attention_bias():
    kj = np.arange(2 * BLOCK)[:, None]
    qi = np.arange(BLOCK)[None, :]
    dist = (BLOCK + qi - kj).astype(np.float32)
    valid = (dist >= 0) & (dist < BLOCK)
    n_heads = N_KV_HEADS * GROUP
    out = np.empty((2 * BLOCK, n_heads * BLOCK), np.float32)
    for head in range(n_heads):
        slope = np.float32(2.0 ** (-8.0 * (head + 1) / n_heads))
        out[:, head * BLOCK:(head + 1) * BLOCK] = np.where(valid, -(slope * dist), np.float32(NEG_INF))
    return out


def _mixer_kernel(sinks_ref, x_ref, g_ref, w_in_hbm, pool_w_ref, pool_b_ref, pool_scale_ref,
                  bias_ref, p_pool_hbm, p_attn_hbm, w_out_hbm, w_up_ref, w_down_ref,
                  o_ref, w_up_o_ref, w_down_o_ref,
                  w_in_ref, p_pool_ref, p_attn_ref, w_out_ref, stage, stage_sem,
                  pool_hist, pool_s2, pool_s4, pool_s8, pool_bd, wq_t, k_ext, vt_ext, q_t, gates, attn_t, *, tm):
    i = pl.program_id(1)
    n_groups = len(POOL_WINDOWS)
    c_pool = n_groups * POOL_GROUP_DIM
    c_q = c_pool + N_KV_HEADS * GROUP * HEAD_DIM
    c_v = c_q + 2 * N_KV_HEADS * HEAD_DIM
    d_model = x_ref.shape[-1]
    n_blocks = tm // BLOCK
    pair = 2 * POOL_GROUP_DIM

    @pl.when(i == 0)
    def _():
        pool_hist[0:POOL_HALO, :] = jnp.zeros((POOL_HALO, c_pool), _F32)
        k_ext[0:BLOCK, :] = jnp.zeros((BLOCK, LANES), _BF16)
        vt_ext[:, 0:BLOCK] = jnp.zeros((LANES, BLOCK), _BF16)

    @pl.when(jnp.logical_and(pl.program_id(0) == 0, i == 0))
    def _():
        srows, scols = stage.shape[1:]
        assert c_q == scols
        chunks = []
        for src, dst in ((w_in_hbm, w_in_ref), (p_pool_hbm, p_pool_ref), (p_attn_hbm, p_attn_ref),
                         (w_out_hbm, w_out_ref)):
            for r in range(0, src.shape[0], srows):
                for c in range(0, src.shape[1], scols):
                    w = min(scols, src.shape[1] - c)
                    has_q = src is w_in_hbm and c == 0
                    keep = c_pool if has_q else w
                    c_dst = c - (c_q - c_pool) if src is w_in_hbm and c > 0 else c
                    chunks.append((src.at[r:r + srows, c:c + w], dst.at[r:r + srows, c_dst:c_dst + keep], r, w,
                                   keep, has_q))

        def chunk_copy(n):
            slot = n % stage.shape[0]
            return pltpu.make_async_copy(chunks[n][0], stage.at[slot, :, 0:chunks[n][3]], stage_sem.at[slot])

        for n in range(min(stage.shape[0], len(chunks))):
            chunk_copy(n).start()
        for n, (_, dst, r, w, keep, has_q) in enumerate(chunks):
            chunk_copy(n).wait()
            vals = stage[n % stage.shape[0], :, 0:w]
            dst[...] = vals[:, 0:keep].astype(_BF16)
            if has_q:
                wq_t[:, r:r + srows] = vals[:, c_pool:c_q].T.astype(_BF16)
            if n + stage.shape[0] < len(chunks):
                chunk_copy(n + stage.shape[0]).start()
        pool_bd[...] = jnp.zeros(pool_bd.shape, _BF16)
        for g in range(n_groups):
            lo = (g % 2) * POOL_GROUP_DIM
            pool_bd[g // 2, lo:lo + POOL_GROUP_DIM, lo:lo + POOL_GROUP_DIM] = pool_w_ref[g].astype(_BF16)

    ub = jnp.concatenate([_rms_norm(x_ref[r:r + BLOCK, :], g_ref[...]).astype(_BF16)
                          for r in range(0, tm, BLOCK)], axis=0)

    c_kv = c_pool + c_v - c_q
    kv = jnp.dot(ub, w_in_ref[:, c_pool:c_kv], preferred_element_type=_F32)
    q_t[...] = (lax.dot_general(wq_t[...], ub, (((1,), (1,)), ((), ())), preferred_element_type=_F32)
                * (1.0 / float(np.sqrt(HEAD_DIM)))).astype(_BF16)
    k_ext[BLOCK:BLOCK + tm, :] = kv[:, 0:LANES].astype(_BF16)
    vt_ext[:, BLOCK:BLOCK + tm] = kv[:, LANES:2 * LANES].T.astype(_BF16)

    n_heads = N_KV_HEADS * GROUP
    krow = lax.broadcasted_iota(jnp.int32, (2 * BLOCK, n_heads * BLOCK), 0)
    pad_keys = jnp.logical_and(krow < BLOCK, i == 0)
    sinks = jnp.concatenate([jnp.full((1, BLOCK), sinks_ref[head], _F32) for head in range(n_heads)], axis=1)
    no_dims = jnp.zeros((HEAD_DIM, GROUP * BLOCK), _BF16)

    def scores(j):
        keys = k_ext[j * BLOCK:(j + 2) * BLOCK, :]
        rhs = jnp.concatenate([
            jnp.concatenate([
                jnp.concatenate([q_t[(h * GROUP + g) * HEAD_DIM:(h * GROUP + g + 1) * HEAD_DIM,
                                     j * BLOCK:(j + 1) * BLOCK] for g in range(GROUP)], axis=1)
                if hh == h else no_dims for hh in range(N_KV_HEADS)], axis=1)
            for h in range(N_KV_HEADS)], axis=0)
        return jnp.dot(keys, rhs, preferred_element_type=_F32)

    def probs(j, s):
        s = s + bias_ref[...]
        if j == 0:
            s = jnp.where(pad_keys, NEG_INF, s)
        m = jnp.maximum(jnp.max(s, axis=0, keepdims=True), sinks)
        p = jnp.exp(s - m)
        denom = jnp.sum(p, axis=0, keepdims=True) + jnp.exp(sinks - m)
        return p.astype(_BF16), 1.0 / denom

    def weighted_values(j, p, inv):
        r0 = j * BLOCK
        for h in range(N_KV_HEADS):
            lanes = slice(h * GROUP * BLOCK, (h + 1) * GROUP * BLOCK)
            vals = vt_ext[h * HEAD_DIM:(h + 1) * HEAD_DIM, r0:r0 + 2 * BLOCK]
            ot = (jnp.dot(vals, p[:, lanes], preferred_element_type=_F32) * inv[:, lanes]).astype(_BF16)
            for g in range(GROUP):
                d0 = (h * GROUP + g) * HEAD_DIM
                attn_t[d0:d0 + HEAD_DIM, r0:r0 + BLOCK] = ot[:, g * BLOCK:(g + 1) * BLOCK]

    def gate_piece(which, r0, nr, c0, nc):
        col = c_kv + which * d_model + c0
        z = jnp.dot(ub[r0:r0 + nr], w_in_ref[:, col:col + nc], preferred_element_type=_F32)
        gates[which, r0:r0 + nr, c0:c0 + nc] = jax.nn.sigmoid(z)

    steps = list(range(n_blocks))
    piece_cols = 2 * LANES
    pieces = [(which, 0, tm, c0, piece_cols) for which in range(2) for c0 in range(0, d_model, piece_cols)]
    lookahead = 2
    pending = [scores(j) for j in steps[:lookahead]]

    rows = tm + POOL_HALO
    pool_hist[POOL_HALO:rows, :] = jnp.dot(ub, w_in_ref[:, 0:c_pool], preferred_element_type=_F32)

    for n, j in enumerate(steps):
        p, inv = probs(j, pending.pop(0))
        if n + lookahead < len(steps):
            pending.append(scores(steps[n + lookahead]))
        for piece in pieces[n * len(pieces) // len(steps):(n + 1) * len(pieces) // len(steps)]:
            gate_piece(*piece)
        weighted_values(j, p, inv)
    k_ext[0:BLOCK, :] = k_ext[tm:tm + BLOCK, :]
    vt_ext[:, 0:BLOCK] = vt_ext[:, tm:tm + BLOCK]

    y_attn = lax.dot_general(attn_t[...], p_attn_ref[...], (((0,), (0,)), ((), ())),
                             preferred_element_type=_F32)

    w_up_o_ref[...] = w_up_ref[...].astype(_BF16)
    w_down_o_ref[...] = w_down_ref[...].astype(_BF16)

    assert POOL_WINDOWS == tuple(2 ** (k + 1) for k in range(n_groups)) and POOL_HALO == 8 * n_groups
    src, lane0, group_sums = pool_hist, 0, []
    for k, dst in enumerate((pool_s2, pool_s4, pool_s8, None)):
        first, shift = 8 * (k + 1), 2 ** k
        window = src[first:rows, lane0:] + src[first - shift:rows - shift, lane0:]
        if dst is None:
            group_sums.append(window)
        else:
            dst[first:rows, :] = window
            group_sums.append(dst[POOL_HALO:rows, 0:POOL_GROUP_DIM])
            src, lane0 = dst, POOL_GROUP_DIM
    wsum = jnp.concatenate(group_sums, axis=1)
    t1 = i * tm + lax.broadcasted_iota(jnp.int32, (tm, c_pool), 0) + 1
    win = jnp.left_shift(2, lax.broadcasted_iota(jnp.int32, (tm, c_pool), 1) // POOL_GROUP_DIM)
    count = jnp.minimum(t1, win).astype(_F32)
    d = (wsum / count - pool_hist[POOL_HALO:rows, :]).astype(_BF16)
    pool_hist[0:POOL_HALO, :] = pool_hist[tm:rows, :]

    y = jnp.concatenate([jnp.dot(d[:, c * pair:(c + 1) * pair], pool_bd[c], preferred_element_type=_F32)
                         for c in range(n_groups // 2)], axis=1)
    y = ((y + pool_b_ref[...]) * pool_scale_ref[...]).astype(_BF16)
    y_pool = jnp.dot(y, p_pool_ref[...], preferred_element_type=_F32)
    nr = tm // 4
    for r0 in range(0, tm, nr):
        mixed = gates[0, r0:r0 + nr] * y_pool[r0:r0 + nr] + gates[1, r0:r0 + nr] * y_attn[r0:r0 + nr]
        o_ref[r0:r0 + nr, :] = x_ref[r0:r0 + nr, :] + jnp.dot(
            mixed.astype(_BF16), w_out_ref[...], preferred_element_type=_F32)


def _mlp_kernel(h_ref, g_ref, w_up_ref, w_down_ref, gf_ref, o_ref, *, ff_chunk, final_norm):
    d_ff = w_up_ref.shape[1]
    tm = h_ref.shape[0]
    ub = jnp.concatenate([_rms_norm(h_ref[r:r + BLOCK, :], g_ref[...]).astype(_BF16)
                          for r in range(0, tm, BLOCK)], axis=0)
    def up(c):
        a = jnp.maximum(jnp.dot(ub, w_up_ref[:, c:c + ff_chunk], preferred_element_type=_F32), 0.0)
        return (a * a).astype(_BF16)

    chunks = list(range(0, d_ff, ff_chunk))
    acc = h_ref[...]
    act = up(chunks[0])
    for n, c in enumerate(chunks[:-1]):
        nxt = up(chunks[n + 1])
        acc = acc + jnp.dot(act, w_down_ref[c:c + ff_chunk, :], preferred_element_type=_F32)
        act = nxt
    nr = tm // 4
    for r0 in range(0, tm, nr):
        out = acc[r0:r0 + nr] + jnp.dot(act[r0:r0 + nr], w_down_ref[chunks[-1]:chunks[-1] + ff_chunk, :],
                                        preferred_element_type=_F32)
        if final_norm:
            out = _rms_norm(out, gf_ref[...])
        o_ref[r0:r0 + nr, :] = out


def _resident(shape):
    zeros = (0,) * len(shape)
    return pl.BlockSpec(shape, lambda *_: zeros, pipeline_mode=pl.Buffered(1))


def _mixer(x, sinks, g, w_in, pool_w, pool_b, pool_scale, bias, p_pool, p_attn, w_out, w_up, w_down):
    b, s, d = x.shape
    tm = MIXER_TILE
    assert s % tm == 0 and tm % BLOCK == 0
    n_i = s // tm
    steps = b * n_i
    assert w_up.shape[0] % (16 * steps) == 0 and w_down.shape[0] % (16 * steps) == 0
    up_rows, down_rows = w_up.shape[0] // steps, w_down.shape[0] // steps
    c_pool = len(POOL_WINDOWS) * POOL_GROUP_DIM
    attn_width = N_KV_HEADS * GROUP * HEAD_DIM
    grid_spec = pltpu.PrefetchScalarGridSpec(
        num_scalar_prefetch=1,
        grid=(b, n_i),
        in_specs=[
            pl.BlockSpec((None, tm, d), lambda bi, i, _: (bi, i, 0)),
            _resident(g.shape), pl.BlockSpec(memory_space=pl.ANY), _resident(pool_w.shape),
            _resident(pool_b.shape), _resident(pool_scale.shape), _resident(bias.shape),
            pl.BlockSpec(memory_space=pl.ANY), pl.BlockSpec(memory_space=pl.ANY), pl.BlockSpec(memory_space=pl.ANY),
            pl.BlockSpec((up_rows, w_up.shape[1]), lambda bi, i, _: (bi * n_i + i, 0)),
            pl.BlockSpec((down_rows, w_down.shape[1]), lambda bi, i, _: (bi * n_i + i, 0)),
        ],
        out_specs=[
            pl.BlockSpec((None, tm, d), lambda bi, i, _: (bi, i, 0)),
            pl.BlockSpec((up_rows, w_up.shape[1]), lambda bi, i, _: (bi * n_i + i, 0)),
            pl.BlockSpec((down_rows, w_down.shape[1]), lambda bi, i, _: (bi * n_i + i, 0)),
        ],
        scratch_shapes=[
            pltpu.VMEM((d, w_in.shape[1] - attn_width), _BF16), pltpu.VMEM(p_pool.shape, _BF16),
            pltpu.VMEM(p_attn.shape, _BF16), pltpu.VMEM(w_out.shape, _BF16),
            pltpu.VMEM((WEIGHT_STAGE_SLOTS,) + WEIGHT_STAGE_BLOCK, _F32),
            pltpu.SemaphoreType.DMA((WEIGHT_STAGE_SLOTS,)),
            pltpu.VMEM((tm + POOL_HALO, c_pool), _F32),
            pltpu.VMEM((tm + POOL_HALO, c_pool), _F32),
            pltpu.VMEM((tm + POOL_HALO, c_pool - POOL_GROUP_DIM), _F32),
            pltpu.VMEM((tm + POOL_HALO, c_pool - 2 * POOL_GROUP_DIM), _F32),
            pltpu.VMEM((len(POOL_WINDOWS) // 2, 2 * POOL_GROUP_DIM, 2 * POOL_GROUP_DIM), _BF16),
            pltpu.VMEM((attn_width, d), _BF16),
            pltpu.VMEM((tm + BLOCK, N_KV_HEADS * HEAD_DIM), _BF16),
            pltpu.VMEM((N_KV_HEADS * HEAD_DIM, tm + BLOCK), _BF16),
            pltpu.VMEM((attn_width, tm), _BF16),
            pltpu.VMEM((2, tm, d), _F32),
            pltpu.VMEM((attn_width, tm), _BF16),
        ],
    )
    return pl.pallas_call(
        functools.partial(_mixer_kernel, tm=tm),
        grid_spec=grid_spec,
        out_shape=[jax.ShapeDtypeStruct(x.shape, _F32),
                   jax.ShapeDtypeStruct(w_up.shape, _BF16), jax.ShapeDtypeStruct(w_down.shape, _BF16)],
        compiler_params=pltpu.CompilerParams(dimension_semantics=("arbitrary", "arbitrary"),
                                             vmem_limit_bytes=MIXER_VMEM_BYTES),
        name="mixer",
    )(sinks, x, g, w_in, pool_w, pool_b, pool_scale, bias, p_pool, p_attn, w_out, w_up, w_down)


def _mlp(h, g, w_up, w_down, g_final, final_norm):
    t, d = h.shape
    tm = MLP_TILE
    assert t % tm == 0
    return pl.pallas_call(
        functools.partial(_mlp_kernel, ff_chunk=1024, final_norm=final_norm),
        grid=(t // tm,),
        in_specs=[
            pl.BlockSpec((tm, d), lambda i: (i, 0)),
            _resident(g.shape), _resident(w_up.shape), _resident(w_down.shape), _resident(g_final.shape),
        ],
        out_specs=pl.BlockSpec((tm, d), lambda i: (i, 0)),
        out_shape=jax.ShapeDtypeStruct(h.shape, _F32),
        compiler_params=pltpu.CompilerParams(dimension_semantics=("arbitrary",), vmem_limit_bytes=MLP_VMEM_BYTES),
        name="mlp",
    )(h, g, w_up, w_down, g_final)


def kernel(x, norm_mix, w_in, pool_w, pool_b, pool_scale, attn_sinks, p_pool, p_attn,
           w_out, norm_mlp, w_up, w_down, norm_final):
    depth = norm_mix.shape[0]
    b, s, d = x.shape
    bias = jnp.asarray(_attention_bias())
    h = x
    for l in range(depth):
        h, w_up_b, w_down_b = _mixer(
            h, attn_sinks[l], norm_mix[l][None, :], w_in[l], pool_w[l], pool_b[l].reshape(1, -1),
            pool_scale[l][None, :], bias, p_pool[l], p_attn[l], w_out[l], w_up[l], w_down[l])
        h = _mlp(h.reshape(b * s, d), norm_mlp[l][None, :], w_up_b, w_down_b,
                 norm_final[None, :], l == depth - 1).reshape(b, s, d)
    return h
```

```python
import functools

import jax
import jax.numpy as jnp
import numpy as np
from jax import lax
from jax.experimental import pallas as pl
from jax.experimental.pallas import tpu as pltpu

HEAD_DIM = 64
N_KV_HEADS = 2
GROUP = 4
BLOCK = 128
POOL_WINDOWS = (2, 4, 8, 16)
POOL_GROUP_DIM = 128
POOL_HALO = 32
RMS_EPS = 1e-5
NEG_INF = -1e30
LANES = 128

MIXER_TILE = 1024
WEIGHT_STAGE_BLOCK = (128, 1024)
WEIGHT_STAGE_SLOTS = 8
MLP_TILE = 1024
MIXER_VMEM_BYTES = 60000 * 1024
MLP_VMEM_BYTES = 48 * 1024 * 1024

_F32 = jnp.float32
_BF16 = jnp.bfloat16


def _rms_norm(x, g):
    return x * lax.rsqrt(jnp.mean(x * x, axis=-1, keepdims=True) + RMS_EPS) * g


def _attention_bias():
    kj = np.arange(2 * BLOCK)[:, None]
    qi = np.arange(BLOCK)[None, :]
    dist = (BLOCK + qi - kj).astype(np.float32)
    valid = (dist >= 0) & (dist < BLOCK)
    n_heads = N_KV_HEADS * GROUP
    out = np.empty((2 * BLOCK, n_heads * BLOCK), np.float32)
    for head in range(n_heads):
        slope = np.float32(2.0 ** (-8.0 * (head + 1) / n_heads))
        out[:, head * BLOCK:(head + 1) * BLOCK] = np.where(valid, -(slope * dist), np.float32(NEG_INF))
    return out


def _mixer_kernel(sinks_ref, x_ref, g_ref, w_in_hbm, pool_w_ref, pool_b_ref, pool_scale_ref,
                  bias_ref, p_pool_hbm, p_attn_hbm, w_out_hbm, w_up_ref, w_down_ref,
                  o_ref, w_up_o_ref, w_down_o_ref,
                  w_in_ref, p_pool_ref, p_attn_ref, w_out_ref, stage, stage_sem,
                  pool_hist, pool_s2, pool_s4, pool_s8, pool_bd, wq_t, k_ext, vt_ext, q_t, gates, attn_t, *, tm):
    i = pl.program_id(1)
    n_groups = len(POOL_WINDOWS)
    c_pool = n_groups * POOL_GROUP_DIM
    c_q = c_pool + N_KV_HEADS * GROUP * HEAD_DIM
    c_v = c_q + 2 * N_KV_HEADS * HEAD_DIM
    d_model = x_ref.shape[-1]
    n_blocks = tm // BLOCK
    pair = 2 * POOL_GROUP_DIM

    @pl.when(i == 0)
    def _():
        pool_hist[0:POOL_HALO, :] = jnp.zeros((POOL_HALO, c_pool), _F32)
        k_ext[0:BLOCK, :] = jnp.zeros((BLOCK, LANES), _BF16)
        vt_ext[:, 0:BLOCK] = jnp.zeros((LANES, BLOCK), _BF16)

    @pl.when(jnp.logical_and(pl.program_id(0) == 0, i == 0))
    def _():
        srows, scols = stage.shape[1:]
        assert c_q == scols
        chunks = []
        for src, dst in ((w_in_hbm, w_in_ref), (p_pool_hbm, p_pool_ref), (p_attn_hbm, p_attn_ref),
                         (w_out_hbm, w_out_ref)):
            for r in range(0, src.shape[0], srows):
                for c in range(0, src.shape[1], scols):
                    w = min(scols, src.shape[1] - c)
                    has_q = src is w_in_hbm and c == 0
                    keep = c_pool if has_q else w
                    c_dst = c - (c_q - c_pool) if src is w_in_hbm and c > 0 else c
                    chunks.append((src.at[r:r + srows, c:c + w], dst.at[r:r + srows, c_dst:c_dst + keep], r, w,
                                   keep, has_q))

        def chunk_copy(n):
            slot = n % stage.shape[0]
            return pltpu.make_async_copy(chunks[n][0], stage.at[slot, :, 0:chunks[n][3]], stage_sem.at[slot])

        for n in range(min(stage.shape[0], len(chunks))):
            chunk_copy(n).start()
        for n, (_, dst, r, w, keep, has_q) in enumerate(chunks):
            chunk_copy(n).wait()
            vals = stage[n % stage.shape[0], :, 0:w]
            dst[...] = vals[:, 0:keep].astype(_BF16)
            if has_q:
                wq_t[:, r:r + srows] = vals[:, c_pool:c_q].T.astype(_BF16)
            if n + stage.shape[0] < len(chunks):
                chunk_copy(n + stage.shape[0]).start()
        pool_bd[...] = jnp.zeros(pool_bd.shape, _BF16)
        for g in range(n_groups):
            lo = (g % 2) * POOL_GROUP_DIM
            pool_bd[g // 2, lo:lo + POOL_GROUP_DIM, lo:lo + POOL_GROUP_DIM] = pool_w_ref[g].astype(_BF16)

    ub = jnp.concatenate([_rms_norm(x_ref[r:r + BLOCK, :], g_ref[...]).astype(_BF16)
                          for r in range(0, tm, BLOCK)], axis=0)

    c_kv = c_pool + c_v - c_q
    kv = jnp.dot(ub, w_in_ref[:, c_pool:c_kv], preferred_element_type=_F32)
    q_t[...] = (lax.dot_general(wq_t[...], ub, (((1,), (1,)), ((), ())), preferred_element_type=_F32)
                * (1.0 / float(np.sqrt(HEAD_DIM)))).astype(_BF16)
    k_ext[BLOCK:BLOCK + tm, :] = kv[:, 0:LANES].astype(_BF16)
    vt_ext[:, BLOCK:BLOCK + tm] = kv[:, LANES:2 * LANES].T.astype(_BF16)

    n_heads = N_KV_HEADS * GROUP
    krow = lax.broadcasted_iota(jnp.int32, (2 * BLOCK, n_heads * BLOCK), 0)
    pad_keys = jnp.logical_and(krow < BLOCK, i == 0)
    sinks = jnp.concatenate([jnp.full((1, BLOCK), sinks_ref[head], _F32) for head in range(n_heads)], axis=1)
    no_dims = jnp.zeros((HEAD_DIM, GROUP * BLOCK), _BF16)

    def scores(j):
        keys = k_ext[j * BLOCK:(j + 2) * BLOCK, :]
        rhs = jnp.concatenate([
            jnp.concatenate([
                jnp.concatenate([q_t[(h * GROUP + g) * HEAD_DIM:(h * GROUP + g + 1) * HEAD_DIM,
                                     j * BLOCK:(j + 1) * BLOCK] for g in range(GROUP)], axis=1)
                if hh == h else no_dims for hh in range(N_KV_HEADS)], axis=1)
            for h in range(N_KV_HEADS)], axis=0)
        return jnp.dot(keys, rhs, preferred_element_type=_F32)

    def probs(j, s):
        s = s + bias_ref[...]
        if j == 0:
            s = jnp.where(pad_keys, NEG_INF, s)
        m = jnp.maximum(jnp.max(s, axis=0, keepdims=True), sinks)
        p = jnp.exp(s - m)
        denom = jnp.sum(p, axis=0, keepdims=True) + jnp.exp(sinks - m)
        return p.astype(_BF16), 1.0 / denom

    def weighted_values(j, p, inv):
        r0 = j * BLOCK
        for h in range(N_KV_HEADS):
            lanes = slice(h * GROUP * BLOCK, (h + 1) * GROUP * BLOCK)
            vals = vt_ext[h * HEAD_DIM:(h + 1) * HEAD_DIM, r0:r0 + 2 * BLOCK]
            ot = (jnp.dot(vals, p[:, lanes], preferred_element_type=_F32) * inv[:, lanes]).astype(_BF16)
            for g in range(GROUP):
                d0 = (h * GROUP + g) * HEAD_DIM
                attn_t[d0:d0 + HEAD_DIM, r0:r0 + BLOCK] = ot[:, g * BLOCK:(g + 1) * BLOCK]

    def gate_piece(which, r0, nr, c0, nc):
        col = c_kv + which * d_model + c0
        z = jnp.dot(ub[r0:r0 + nr], w_in_ref[:, col:col + nc], preferred_element_type=_F32)
        gates[which, r0:r0 + nr, c0:c0 + nc] = jax.nn.sigmoid(z)

    steps = list(range(n_blocks))
    piece_cols = 2 * LANES
    pieces = [(which, 0, tm, c0, piece_cols) for which in range(2) for c0 in range(0, d_model, piece_cols)]
    lookahead = 2
    pending = [scores(j) for j in steps[:lookahead]]

    rows = tm + POOL_HALO
    pool_hist[POOL_HALO:rows, :] = jnp.dot(ub, w_in_ref[:, 0:c_pool], preferred_element_type=_F32)

    for n, j in enumerate(steps):
        p, inv = probs(j, pending.pop(0))
        if n + lookahead < len(steps):
            pending.append(scores(steps[n + lookahead]))
        for piece in pieces[n * len(pieces) // len(steps):(n + 1) * len(pieces) // len(steps)]:
            gate_piece(*piece)
        weighted_values(j, p, inv)
    k_ext[0:BLOCK, :] = k_ext[tm:tm + BLOCK, :]
    vt_ext[:, 0:BLOCK] = vt_ext[:, tm:tm + BLOCK]

    y_attn = lax.dot_general(attn_t[...], p_attn_ref[...], (((0,), (0,)), ((), ())),
                             preferred_element_type=_F32)

    w_up_o_ref[...] = w_up_ref[...].astype(_BF16)
    w_down_o_ref[...] = w_down_ref[...].astype(_BF16)

    assert POOL_WINDOWS == tuple(2 ** (k + 1) for k in range(n_groups)) and POOL_HALO == 8 * n_groups
    src, lane0, group_sums = pool_hist, 0, []
    for k, dst in enumerate((pool_s2, pool_s4, pool_s8, None)):
        first, shift = 8 * (k + 1), 2 ** k
        window = src[first:rows, lane0:] + src[first - shift:rows - shift, lane0:]
        if dst is None:
            group_sums.append(window)
        else:
            dst[first:rows, :] = window
            group_sums.append(dst[POOL_HALO:rows, 0:POOL_GROUP_DIM])
            src, lane0 = dst, POOL_GROUP_DIM
    wsum = jnp.concatenate(group_sums, axis=1)
    t1 = i * tm + lax.broadcasted_iota(jnp.int32, (tm, c_pool), 0) + 1
    win = jnp.left_shift(2, lax.broadcasted_iota(jnp.int32, (tm, c_pool), 1) // POOL_GROUP_DIM)
    count = jnp.minimum(t1, win).astype(_F32)
    d = (wsum / count - pool_hist[POOL_HALO:rows, :]).astype(_BF16)
    pool_hist[0:POOL_HALO, :] = pool_hist[tm:rows, :]

    y = jnp.concatenate([jnp.dot(d[:, c * pair:(c + 1) * pair], pool_bd[c], preferred_element_type=_F32)
                         for c in range(n_groups // 2)], axis=1)
    y = ((y + pool_b_ref[...]) * pool_scale_ref[...]).astype(_BF16)
    y_pool = jnp.dot(y, p_pool_ref[...], preferred_element_type=_F32)
    nr = tm // 4
    for r0 in range(0, tm, nr):
        mixed = gates[0, r0:r0 + nr] * y_pool[r0:r0 + nr] + gates[1, r0:r0 + nr] * y_attn[r0:r0 + nr]
        o_ref[r0:r0 + nr, :] = x_ref[r0:r0 + nr, :] + jnp.dot(
            mixed.astype(_BF16), w_out_ref[...], preferred_element_type=_F32)


def _mlp_kernel(h_ref, g_ref, w_up_ref, w_down_ref, gf_ref, o_ref, *, ff_chunk, final_norm):
    d_ff = w_up_ref.shape[1]
    tm = h_ref.shape[0]
    ub = jnp.concatenate([_rms_norm(h_ref[r:r + BLOCK, :], g_ref[...]).astype(_BF16)
                          for r in range(0, tm, BLOCK)], axis=0)
    def up(c):
        a = jnp.maximum(jnp.dot(ub, w_up_ref[:, c:c + ff_chunk], preferred_element_type=_F32), 0.0)
        return (a * a).astype(_BF16)

    chunks = list(range(0, d_ff, ff_chunk))
    acc = h_ref[...]
    act = up(chunks[0])
    for n, c in enumerate(chunks[:-1]):
        nxt = up(chunks[n + 1])
        acc = acc + jnp.dot(act, w_down_ref[c:c + ff_chunk, :], preferred_element_type=_F32)
        act = nxt
    nr = tm // 4
    for r0 in range(0, tm, nr):
        out = acc[r0:r0 + nr] + jnp.dot(act[r0:r0 + nr], w_down_ref[chunks[-1]:chunks[-1] + ff_chunk, :],
                                        preferred_element_type=_F32)
        if final_norm:
            out = _rms_norm(out, gf_ref[...])
        o_ref[r0:r0 + nr, :] = out


def _resident(shape):
    zeros = (0,) * len(shape)
    return pl.BlockSpec(shape, lambda *_: zeros, pipeline_mode=pl.Buffered(1))


def _mixer(x, sinks, g, w_in, pool_w, pool_b, pool_scale, bias, p_pool, p_attn, w_out, w_up, w_down):
    b, s, d = x.shape
    tm = MIXER_TILE
    assert s % tm == 0 and tm % BLOCK == 0
    n_i = s // tm
    d_ff = w_up.shape[1]
    assert d_ff % (b * n_i * LANES) == 0
    ff_slice = d_ff // (b * n_i)
    c_pool = len(POOL_WINDOWS) * POOL_GROUP_DIM
    attn_width = N_KV_HEADS * GROUP * HEAD_DIM
    grid_spec = pltpu.PrefetchScalarGridSpec(
        num_scalar_prefetch=1,
        grid=(b, n_i),
        in_specs=[
            pl.BlockSpec((None, tm, d), lambda bi, i, _: (bi, i, 0)),
            _resident(g.shape), pl.BlockSpec(memory_space=pl.ANY), _resident(pool_w.shape),
            _resident(pool_b.shape), _resident(pool_scale.shape), _resident(bias.shape),
            pl.BlockSpec(memory_space=pl.ANY), pl.BlockSpec(memory_space=pl.ANY), pl.BlockSpec(memory_space=pl.ANY),
            pl.BlockSpec((d, ff_slice), lambda bi, i, _: (0, bi * n_i + i)),
            pl.BlockSpec((ff_slice, d), lambda bi, i, _: (bi * n_i + i, 0)),
        ],
        out_specs=[
            pl.BlockSpec((None, tm, d), lambda bi, i, _: (bi, i, 0)),
            pl.BlockSpec((d, ff_slice), lambda bi, i, _: (0, bi * n_i + i)),
            pl.BlockSpec((ff_slice, d), lambda bi, i, _: (bi * n_i + i, 0)),
        ],
        scratch_shapes=[
            pltpu.VMEM((d, w_in.shape[1] - attn_width), _BF16), pltpu.VMEM(p_pool.shape, _BF16),
            pltpu.VMEM(p_attn.shape, _BF16), pltpu.VMEM(w_out.shape, _BF16),
            pltpu.VMEM((WEIGHT_STAGE_SLOTS,) + WEIGHT_STAGE_BLOCK, _F32),
            pltpu.SemaphoreType.DMA((WEIGHT_STAGE_SLOTS,)),
            pltpu.VMEM((tm + POOL_HALO, c_pool), _F32),
            pltpu.VMEM((tm + POOL_HALO, c_pool), _F32),
            pltpu.VMEM((tm + POOL_HALO, c_pool - POOL_GROUP_DIM), _F32),
            pltpu.VMEM((tm + POOL_HALO, c_pool - 2 * POOL_GROUP_DIM), _F32),
            pltpu.VMEM((len(POOL_WINDOWS) // 2, 2 * POOL_GROUP_DIM, 2 * POOL_GROUP_DIM), _BF16),
            pltpu.VMEM((attn_width, d), _BF16),
            pltpu.VMEM((tm + BLOCK, N_KV_HEADS * HEAD_DIM), _BF16),
            pltpu.VMEM((N_KV_HEADS * HEAD_DIM, tm + BLOCK), _BF16),
            pltpu.VMEM((attn_width, tm), _BF16),
            pltpu.VMEM((2, tm, d), _F32),
            pltpu.VMEM((attn_width, tm), _BF16),
        ],
    )
    return pl.pallas_call(
        functools.partial(_mixer_kernel, tm=tm),
        grid_spec=grid_spec,
        out_shape=[jax.ShapeDtypeStruct(x.shape, _F32),
                   jax.ShapeDtypeStruct(w_up.shape, _BF16), jax.ShapeDtypeStruct(w_down.shape, _BF16)],
        compiler_params=pltpu.CompilerParams(dimension_semantics=("arbitrary", "arbitrary"),
                                             vmem_limit_bytes=MIXER_VMEM_BYTES),
        name="mixer",
    )(sinks, x, g, w_in, pool_w, pool_b, pool_scale, bias, p_pool, p_attn, w_out, w_up, w_down)


def _mlp(h, g, w_up, w_down, g_final, final_norm):
    t, d = h.shape
    tm = MLP_TILE
    assert t % tm == 0
    return pl.pallas_call(
        functools.partial(_mlp_kernel, ff_chunk=1024, final_norm=final_norm),
        grid=(t // tm,),
        in_specs=[
            pl.BlockSpec((tm, d), lambda i: (i, 0)),
            _resident(g.shape), _resident(w_up.shape), _resident(w_down.shape), _resident(g_final.shape),
        ],
        out_specs=pl.BlockSpec((tm, d), lambda i: (i, 0)),
        out_shape=jax.ShapeDtypeStruct(h.shape, _F32),
        compiler_params=pltpu.CompilerParams(dimension_semantics=("arbitrary",), vmem_limit_bytes=MLP_VMEM_BYTES),
        name="mlp",
    )(h, g, w_up, w_down, g_final)


def kernel(x, norm_mix, w_in, pool_w, pool_b, pool_scale, attn_sinks, p_pool, p_attn,
           w_out, norm_mlp, w_up, w_down, norm_final):
    depth = norm_mix.shape[0]
    b, s, d = x.shape
    bias = jnp.asarray(_attention_bias())
    h = x
    for l in range(depth):
        h, w_up_b, w_down_b = _mixer(
            h, attn_sinks[l], norm_mix[l][None, :], w_in[l], pool_w[l], pool_b[l].reshape(1, -1),
            pool_scale[l][None, :], bias, p_pool[l], p_attn[l], w_out[l], w_up[l], w_down[l])
        h = _mlp(h.reshape(b * s, d), norm_mlp[l][None, :], w_up_b, w_down_b,
                 norm_final[None, :], l == depth - 1).reshape(b, s, d)
    return h
```

```python
import functools

import jax
import jax.numpy as jnp
import numpy as np
from jax import lax
from jax.experimental import pallas as pl
from jax.experimental.pallas import tpu as pltpu

HEAD_DIM = 64
N_KV_HEADS = 2
GROUP = 4
BLOCK = 128
POOL_WINDOWS = (2, 4, 8, 16)
POOL_GROUP_DIM = 128
POOL_HALO = 32
RMS_EPS = 1e-5
NEG_INF = -1e30
LANES = 128

MIXER_TILE = 1024
WEIGHT_STAGE_BLOCK = (128, 1024)
WEIGHT_STAGE_SLOTS = 8
MLP_TILE = 1024
MIXER_VMEM_BYTES = 60000 * 1024
MLP_VMEM_BYTES = 48 * 1024 * 1024

_F32 = jnp.float32
_BF16 = jnp.bfloat16


def _rms_norm(x, g):
    return x * lax.rsqrt(jnp.mean(x * x, axis=-1, keepdims=True) + RMS_EPS) * g


def _attention_bias():
    kj = np.arange(2 * BLOCK)[:, None]
    qi = np.arange(BLOCK)[None, :]
    dist = (BLOCK + qi - kj).astype(np.float32)
    valid = (dist >= 0) & (dist < BLOCK)
    n_heads = N_KV_HEADS * GROUP
    out = np.empty((2 * BLOCK, n_heads * BLOCK), np.float32)
    for head in range(n_heads):
        slope = np.float32(2.0 ** (-8.0 * (head + 1) / n_heads))
        out[:, head * BLOCK:(head + 1) * BLOCK] = np.where(valid, -(slope * dist), np.float32(NEG_INF))
    return out


def _mixer_kernel(sinks_ref, x_ref, g_ref, w_in_hbm, pool_w_ref, pool_b_ref, pool_scale_ref,
                  bias_ref, p_pool_hbm, p_attn_hbm, w_out_hbm, w_up_ref, w_down_ref,
                  o_ref, w_up_o_hbm, w_down_o_hbm,
                  w_in_ref, p_pool_ref, p_attn_ref, w_out_ref, stage, stage_sem,
                  pool_hist, pool_s2, pool_s4, pool_s8, pool_bd, wq_t, k_ext, vt_ext, q_t, gates, attn_t,
                  up_cast, down_cast, cast_sem, *, tm):
    i = pl.program_id(1)
    n_groups = len(POOL_WINDOWS)
    c_pool = n_groups * POOL_GROUP_DIM
    c_q = c_pool + N_KV_HEADS * GROUP * HEAD_DIM
    c_v = c_q + 2 * N_KV_HEADS * HEAD_DIM
    d_model = x_ref.shape[-1]
    n_blocks = tm // BLOCK
    pair = 2 * POOL_GROUP_DIM

    @pl.when(i == 0)
    def _():
        pool_hist[0:POOL_HALO, :] = jnp.zeros((POOL_HALO, c_pool), _F32)
        k_ext[0:BLOCK, :] = jnp.zeros((BLOCK, LANES), _BF16)
        vt_ext[:, 0:BLOCK] = jnp.zeros((LANES, BLOCK), _BF16)

    @pl.when(jnp.logical_and(pl.program_id(0) == 0, i == 0))
    def _():
        srows, scols = stage.shape[1:]
        assert c_q == scols
        chunks = []
        for src, dst in ((w_in_hbm, w_in_ref), (p_pool_hbm, p_pool_ref), (p_attn_hbm, p_attn_ref),
                         (w_out_hbm, w_out_ref)):
            for r in range(0, src.shape[0], srows):
                for c in range(0, src.shape[1], scols):
                    w = min(scols, src.shape[1] - c)
                    has_q = src is w_in_hbm and c == 0
                    keep = c_pool if has_q else w
                    c_dst = c - (c_q - c_pool) if src is w_in_hbm and c > 0 else c
                    chunks.append((src.at[r:r + srows, c:c + w], dst.at[r:r + srows, c_dst:c_dst + keep], r, w,
                                   keep, has_q))

        def chunk_copy(n):
            slot = n % stage.shape[0]
            return pltpu.make_async_copy(chunks[n][0], stage.at[slot, :, 0:chunks[n][3]], stage_sem.at[slot])

        for n in range(min(stage.shape[0], len(chunks))):
            chunk_copy(n).start()
        for n, (_, dst, r, w, keep, has_q) in enumerate(chunks):
            chunk_copy(n).wait()
            vals = stage[n % stage.shape[0], :, 0:w]
            dst[...] = vals[:, 0:keep].astype(_BF16)
            if has_q:
                wq_t[:, r:r + srows] = vals[:, c_pool:c_q].T.astype(_BF16)
            if n + stage.shape[0] < len(chunks):
                chunk_copy(n + stage.shape[0]).start()
        pool_bd[...] = jnp.zeros(pool_bd.shape, _BF16)
        for g in range(n_groups):
            lo = (g % 2) * POOL_GROUP_DIM
            pool_bd[g // 2, lo:lo + POOL_GROUP_DIM, lo:lo + POOL_GROUP_DIM] = pool_w_ref[g].astype(_BF16)

    ub = jnp.concatenate([_rms_norm(x_ref[r:r + BLOCK, :], g_ref[...]).astype(_BF16)
                          for r in range(0, tm, BLOCK)], axis=0)

    c_kv = c_pool + c_v - c_q
    kv = jnp.dot(ub, w_in_ref[:, c_pool:c_kv], preferred_element_type=_F32)
    q_t[...] = (lax.dot_general(wq_t[...], ub, (((1,), (1,)), ((), ())), preferred_element_type=_F32)
                * (1.0 / float(np.sqrt(HEAD_DIM)))).astype(_BF16)
    k_ext[BLOCK:BLOCK + tm, :] = kv[:, 0:LANES].astype(_BF16)
    vt_ext[:, BLOCK:BLOCK + tm] = kv[:, LANES:2 * LANES].T.astype(_BF16)

    n_heads = N_KV_HEADS * GROUP
    krow = lax.broadcasted_iota(jnp.int32, (2 * BLOCK, n_heads * BLOCK), 0)
    pad_keys = jnp.logical_and(krow < BLOCK, i == 0)
    sinks = jnp.concatenate([jnp.full((1, BLOCK), sinks_ref[head], _F32) for head in range(n_heads)], axis=1)
    no_dims = jnp.zeros((HEAD_DIM, GROUP * BLOCK), _BF16)

    def scores(j):
        keys = k_ext[j * BLOCK:(j + 2) * BLOCK, :]
        rhs = jnp.concatenate([
            jnp.concatenate([
                jnp.concatenate([q_t[(h * GROUP + g) * HEAD_DIM:(h * GROUP + g + 1) * HEAD_DIM,
                                     j * BLOCK:(j + 1) * BLOCK] for g in range(GROUP)], axis=1)
                if hh == h else no_dims for hh in range(N_KV_HEADS)], axis=1)
            for h in range(N_KV_HEADS)], axis=0)
        return jnp.dot(keys, rhs, preferred_element_type=_F32)

    def probs(j, s):
        s = s + bias_ref[...]
        if j == 0:
            s = jnp.where(pad_keys, NEG_INF, s)
        m = jnp.maximum(jnp.max(s, axis=0, keepdims=True), sinks)
        p = jnp.exp(s - m)
        denom = jnp.sum(p, axis=0, keepdims=True) + jnp.exp(sinks - m)
        return p.astype(_BF16), 1.0 / denom

    def weighted_values(j, p, inv):
        r0 = j * BLOCK
        for h in range(N_KV_HEADS):
            lanes = slice(h * GROUP * BLOCK, (h + 1) * GROUP * BLOCK)
            vals = vt_ext[h * HEAD_DIM:(h + 1) * HEAD_DIM, r0:r0 + 2 * BLOCK]
            ot = (jnp.dot(vals, p[:, lanes], preferred_element_type=_F32) * inv[:, lanes]).astype(_BF16)
            for g in range(GROUP):
                d0 = (h * GROUP + g) * HEAD_DIM
                attn_t[d0:d0 + HEAD_DIM, r0:r0 + BLOCK] = ot[:, g * BLOCK:(g + 1) * BLOCK]

    def gate_piece(which, r0, nr, c0, nc):
        col = c_kv + which * d_model + c0
        z = jnp.dot(ub[r0:r0 + nr], w_in_ref[:, col:col + nc], preferred_element_type=_F32)
        gates[which, r0:r0 + nr, c0:c0 + nc] = jax.nn.sigmoid(z)

    steps = list(range(n_blocks))
    piece_cols = 2 * LANES
    pieces = [(which, 0, tm, c0, piece_cols) for which in range(2) for c0 in range(0, d_model, piece_cols)]
    lookahead = 2
    pending = [scores(j) for j in steps[:lookahead]]

    rows = tm + POOL_HALO
    pool_hist[POOL_HALO:rows, :] = jnp.dot(ub, w_in_ref[:, 0:c_pool], preferred_element_type=_F32)

    for n, j in enumerate(steps):
        p, inv = probs(j, pending.pop(0))
        if n + lookahead < len(steps):
            pending.append(scores(steps[n + lookahead]))
        for piece in pieces[n * len(pieces) // len(steps):(n + 1) * len(pieces) // len(steps)]:
            gate_piece(*piece)
        weighted_values(j, p, inv)
    k_ext[0:BLOCK, :] = k_ext[tm:tm + BLOCK, :]
    vt_ext[:, 0:BLOCK] = vt_ext[:, tm:tm + BLOCK]

    y_attn = lax.dot_general(attn_t[...], p_attn_ref[...], (((0,), (0,)), ((), ())),
                             preferred_element_type=_F32)


    assert POOL_WINDOWS == tuple(2 ** (k + 1) for k in range(n_groups)) and POOL_HALO == 8 * n_groups
    src, lane0, group_sums = pool_hist, 0, []
    for k, dst in enumerate((pool_s2, pool_s4, pool_s8, None)):
        first, shift = 8 * (k + 1), 2 ** k
        window = src[first:rows, lane0:] + src[first - shift:rows - shift, lane0:]
        if dst is None:
            group_sums.append(window)
        else:
            dst[first:rows, :] = window
            group_sums.append(dst[POOL_HALO:rows, 0:POOL_GROUP_DIM])
            src, lane0 = dst, POOL_GROUP_DIM
    wsum = jnp.concatenate(group_sums, axis=1)
    t1 = i * tm + lax.broadcasted_iota(jnp.int32, (tm, c_pool), 0) + 1
    win = jnp.left_shift(2, lax.broadcasted_iota(jnp.int32, (tm, c_pool), 1) // POOL_GROUP_DIM)
    count = jnp.minimum(t1, win).astype(_F32)
    d = (wsum / count - pool_hist[POOL_HALO:rows, :]).astype(_BF16)
    pool_hist[0:POOL_HALO, :] = pool_hist[tm:rows, :]

    y = jnp.concatenate([jnp.dot(d[:, c * pair:(c + 1) * pair], pool_bd[c], preferred_element_type=_F32)
                         for c in range(n_groups // 2)], axis=1)
    y = ((y + pool_b_ref[...]) * pool_scale_ref[...]).astype(_BF16)
    y_pool = jnp.dot(y, p_pool_ref[...], preferred_element_type=_F32)
    step = pl.program_id(0) * pl.num_programs(1) + i
    up_cast[...] = w_up_ref[...].astype(_BF16)
    down_cast[...] = w_down_ref[...].astype(_BF16)
    slab_copies = [
        pltpu.make_async_copy(up_cast, w_up_o_hbm.at[pl.ds(step * up_cast.shape[0], up_cast.shape[0]), :],
                              cast_sem.at[0]),
        pltpu.make_async_copy(down_cast, w_down_o_hbm.at[pl.ds(step * down_cast.shape[0], down_cast.shape[0]), :],
                              cast_sem.at[1]),
    ]
    for copy in slab_copies:
        copy.start()

    nr = tm // 4
    for r0 in range(0, tm, nr):
        mixed = gates[0, r0:r0 + nr] * y_pool[r0:r0 + nr] + gates[1, r0:r0 + nr] * y_attn[r0:r0 + nr]
        o_ref[r0:r0 + nr, :] = x_ref[r0:r0 + nr, :] + jnp.dot(
            mixed.astype(_BF16), w_out_ref[...], preferred_element_type=_F32)
    for copy in slab_copies:
        copy.wait()


def _mlp_kernel(h_ref, g_ref, w_up_ref, w_down_ref, gf_ref, o_ref, *, ff_chunk, final_norm):
    d_ff = w_up_ref.shape[1]
    tm = h_ref.shape[0]
    ub = jnp.concatenate([_rms_norm(h_ref[r:r + BLOCK, :], g_ref[...]).astype(_BF16)
                          for r in range(0, tm, BLOCK)], axis=0)
    def up(c):
        a = jnp.maximum(jnp.dot(ub, w_up_ref[:, c:c + ff_chunk], preferred_element_type=_F32), 0.0)
        return (a * a).astype(_BF16)

    chunks = list(range(0, d_ff, ff_chunk))
    acc = h_ref[...]
    act = up(chunks[0])
    for n, c in enumerate(chunks[:-1]):
        nxt = up(chunks[n + 1])
        acc = acc + jnp.dot(act, w_down_ref[c:c + ff_chunk, :], preferred_element_type=_F32)
        act = nxt
    nr = tm // 4
    for r0 in range(0, tm, nr):
        out = acc[r0:r0 + nr] + jnp.dot(act[r0:r0 + nr], w_down_ref[chunks[-1]:chunks[-1] + ff_chunk, :],
                                        preferred_element_type=_F32)
        if final_norm:
            out = _rms_norm(out, gf_ref[...])
        o_ref[r0:r0 + nr, :] = out


def _resident(shape):
    zeros = (0,) * len(shape)
    return pl.BlockSpec(shape, lambda *_: zeros, pipeline_mode=pl.Buffered(1))


def _mixer(x, sinks, g, w_in, pool_w, pool_b, pool_scale, bias, p_pool, p_attn, w_out, w_up, w_down):
    b, s, d = x.shape
    tm = MIXER_TILE
    assert s % tm == 0 and tm % BLOCK == 0
    n_i = s // tm
    steps = b * n_i
    assert w_up.shape[0] % (16 * steps) == 0 and w_down.shape[0] % (16 * steps) == 0
    up_rows, down_rows = w_up.shape[0] // steps, w_down.shape[0] // steps
    c_pool = len(POOL_WINDOWS) * POOL_GROUP_DIM
    attn_width = N_KV_HEADS * GROUP * HEAD_DIM
    grid_spec = pltpu.PrefetchScalarGridSpec(
        num_scalar_prefetch=1,
        grid=(b, n_i),
        in_specs=[
            pl.BlockSpec((None, tm, d), lambda bi, i, _: (bi, i, 0)),
            _resident(g.shape), pl.BlockSpec(memory_space=pl.ANY), _resident(pool_w.shape),
            _resident(pool_b.shape), _resident(pool_scale.shape), _resident(bias.shape),
            pl.BlockSpec(memory_space=pl.ANY), pl.BlockSpec(memory_space=pl.ANY), pl.BlockSpec(memory_space=pl.ANY),
            pl.BlockSpec((up_rows, w_up.shape[1]), lambda bi, i, _: (bi * n_i + i, 0)),
            pl.BlockSpec((down_rows, w_down.shape[1]), lambda bi, i, _: (bi * n_i + i, 0)),
        ],
        out_specs=[
            pl.BlockSpec((None, tm, d), lambda bi, i, _: (bi, i, 0)),
            pl.BlockSpec(memory_space=pl.ANY), pl.BlockSpec(memory_space=pl.ANY),
        ],
        scratch_shapes=[
            pltpu.VMEM((d, w_in.shape[1] - attn_width), _BF16), pltpu.VMEM(p_pool.shape, _BF16),
            pltpu.VMEM(p_attn.shape, _BF16), pltpu.VMEM(w_out.shape, _BF16),
            pltpu.VMEM((WEIGHT_STAGE_SLOTS,) + WEIGHT_STAGE_BLOCK, _F32),
            pltpu.SemaphoreType.DMA((WEIGHT_STAGE_SLOTS,)),
            pltpu.VMEM((tm + POOL_HALO, c_pool), _F32),
            pltpu.VMEM((tm + POOL_HALO, c_pool), _F32),
            pltpu.VMEM((tm + POOL_HALO, c_pool - POOL_GROUP_DIM), _F32),
            pltpu.VMEM((tm + POOL_HALO, c_pool - 2 * POOL_GROUP_DIM), _F32),
            pltpu.VMEM((len(POOL_WINDOWS) // 2, 2 * POOL_GROUP_DIM, 2 * POOL_GROUP_DIM), _BF16),
            pltpu.VMEM((attn_width, d), _BF16),
            pltpu.VMEM((tm + BLOCK, N_KV_HEADS * HEAD_DIM), _BF16),
            pltpu.VMEM((N_KV_HEADS * HEAD_DIM, tm + BLOCK), _BF16),
            pltpu.VMEM((attn_width, tm), _BF16),
            pltpu.VMEM((2, tm, d), _F32),
            pltpu.VMEM((attn_width, tm), _BF16),
            pltpu.VMEM((up_rows, w_up.shape[1]), _BF16),
            pltpu.VMEM((down_rows, w_down.shape[1]), _BF16),
            pltpu.SemaphoreType.DMA((2,)),
        ],
    )
    return pl.pallas_call(
        functools.partial(_mixer_kernel, tm=tm),
        grid_spec=grid_spec,
        out_shape=[jax.ShapeDtypeStruct(x.shape, _F32),
                   jax.ShapeDtypeStruct(w_up.shape, _BF16), jax.ShapeDtypeStruct(w_down.shape, _BF16)],
        compiler_params=pltpu.CompilerParams(dimension_semantics=("arbitrary", "arbitrary"),
                                             vmem_limit_bytes=MIXER_VMEM_BYTES),
        name="mixer",
    )(sinks, x, g, w_in, pool_w, pool_b, pool_scale, bias, p_pool, p_attn, w_out, w_up, w_down)


def _mlp(h, g, w_up, w_down, g_final, final_norm):
    t, d = h.shape
    tm = MLP_TILE
    assert t % tm == 0
    return pl.pallas_call(
        functools.partial(_mlp_kernel, ff_chunk=1024, final_norm=final_norm),
        grid=(t // tm,),
        in_specs=[
            pl.BlockSpec((tm, d), lambda i: (i, 0)),
            _resident(g.shape), _resident(w_up.shape), _resident(w_down.shape), _resident(g_final.shape),
        ],
        out_specs=pl.BlockSpec((tm, d), lambda i: (i, 0)),
        out_shape=jax.ShapeDtypeStruct(h.shape, _F32),
        compiler_params=pltpu.CompilerParams(dimension_semantics=("arbitrary",), vmem_limit_bytes=MLP_VMEM_BYTES),
        name="mlp",
    )(h, g, w_up, w_down, g_final)


def kernel(x, norm_mix, w_in, pool_w, pool_b, pool_scale, attn_sinks, p_pool, p_attn,
           w_out, norm_mlp, w_up, w_down, norm_final):
    depth = norm_mix.shape[0]
    b, s, d = x.shape
    bias = jnp.asarray(_attention_bias())
    h = x
    for l in range(depth):
        h, w_up_b, w_down_b = _mixer(
            h, attn_sinks[l], norm_mix[l][None, :], w_in[l], pool_w[l], pool_b[l].reshape(1, -1),
            pool_scale[l][None, :], bias, p_pool[l], p_attn[l], w_out[l], w_up[l], w_down[l])
        h = _mlp(h.reshape(b * s, d), norm_mlp[l][None, :], w_up_b, w_down_b,
                 norm_final[None, :], l == depth - 1).reshape(b, s, d)
    return h
```

```python
import functools

import jax
import jax.numpy as jnp
import numpy as np
from jax import lax
from jax.experimental import pallas as pl
from jax.experimental.pallas import tpu as pltpu

HEAD_DIM = 64
N_KV_HEADS = 2
GROUP = 4
BLOCK = 128
POOL_WINDOWS = (2, 4, 8, 16)
POOL_GROUP_DIM = 128
POOL_HALO = 32
RMS_EPS = 1e-5
NEG_INF = -1e30
LANES = 128

MIXER_TILE = 1024
WEIGHT_STAGE_BLOCK = (128, 1024)
WEIGHT_STAGE_SLOTS = 8
MLP_TILE = 1024
MIXER_VMEM_BYTES = 60000 * 1024
MLP_VMEM_BYTES = 56 * 1024 * 1024

_F32 = jnp.float32
_BF16 = jnp.bfloat16


def _rms_norm(x, g):
    return x * lax.rsqrt(jnp.mean(x * x, axis=-1, keepdims=True) + RMS_EPS) * g


def _attention_bias():
    kj = np.arange(2 * BLOCK)[:, None]
    qi = np.arange(BLOCK)[None, :]
    dist = (BLOCK + qi - kj).astype(np.float32)
    valid = (dist >= 0) & (dist < BLOCK)
    n_heads = N_KV_HEADS * GROUP
    out = np.empty((2 * BLOCK, n_heads * BLOCK), np.float32)
    for head in range(n_heads):
        slope = np.float32(2.0 ** (-8.0 * (head + 1) / n_heads))
        out[:, head * BLOCK:(head + 1) * BLOCK] = np.where(valid, -(slope * dist), np.float32(NEG_INF))
    return out


def _mixer_kernel(sinks_ref, x_ref, g_ref, w_in_hbm, pool_w_ref, pool_b_ref, pool_scale_ref,
                  bias_ref, p_pool_hbm, p_attn_hbm, w_out_hbm, w_up_ref, w_down_ref,
                  o_ref, w_up_o_ref, w_down_o_ref,
                  w_in_ref, p_pool_ref, p_attn_ref, w_out_ref, stage, stage_sem,
                  pool_hist, pool_s2, pool_s4, pool_s8, pool_bd, wq_t, k_ext, vt_ext, q_t, gates, attn_t, *, tm):
    i = pl.program_id(1)
    n_groups = len(POOL_WINDOWS)
    c_pool = n_groups * POOL_GROUP_DIM
    c_q = c_pool + N_KV_HEADS * GROUP * HEAD_DIM
    c_v = c_q + 2 * N_KV_HEADS * HEAD_DIM
    d_model = x_ref.shape[-1]
    n_blocks = tm // BLOCK
    pair = 2 * POOL_GROUP_DIM

    @pl.when(i == 0)
    def _():
        pool_hist[0:POOL_HALO, :] = jnp.zeros((POOL_HALO, c_pool), _F32)
        k_ext[0:BLOCK, :] = jnp.zeros((BLOCK, LANES), _BF16)
        vt_ext[:, 0:BLOCK] = jnp.zeros((LANES, BLOCK), _BF16)

    @pl.when(jnp.logical_and(pl.program_id(0) == 0, i == 0))
    def _():
        srows, scols = stage.shape[1:]
        assert c_q == scols
        chunks = []
        for src, dst in ((w_in_hbm, w_in_ref), (p_pool_hbm, p_pool_ref), (p_attn_hbm, p_attn_ref),
                         (w_out_hbm, w_out_ref)):
            for r in range(0, src.shape[0], srows):
                for c in range(0, src.shape[1], scols):
                    w = min(scols, src.shape[1] - c)
                    has_q = src is w_in_hbm and c == 0
                    keep = c_pool if has_q else w
                    c_dst = c - (c_q - c_pool) if src is w_in_hbm and c > 0 else c
                    chunks.append((src.at[r:r + srows, c:c + w], dst.at[r:r + srows, c_dst:c_dst + keep], r, w,
                                   keep, has_q))

        def chunk_copy(n):
            slot = n % stage.shape[0]
            return pltpu.make_async_copy(chunks[n][0], stage.at[slot, :, 0:chunks[n][3]], stage_sem.at[slot])

        for n in range(min(stage.shape[0], len(chunks))):
            chunk_copy(n).start()
        for n, (_, dst, r, w, keep, has_q) in enumerate(chunks):
            chunk_copy(n).wait()
            vals = stage[n % stage.shape[0], :, 0:w]
            dst[...] = vals[:, 0:keep].astype(_BF16)
            if has_q:
                wq_t[:, r:r + srows] = vals[:, c_pool:c_q].T.astype(_BF16)
            if n + stage.shape[0] < len(chunks):
                chunk_copy(n + stage.shape[0]).start()
        pool_bd[...] = jnp.zeros(pool_bd.shape, _BF16)
        for g in range(n_groups):
            lo = (g % 2) * POOL_GROUP_DIM
            pool_bd[g // 2, lo:lo + POOL_GROUP_DIM, lo:lo + POOL_GROUP_DIM] = pool_w_ref[g].astype(_BF16)

    ub = jnp.concatenate([_rms_norm(x_ref[r:r + BLOCK, :], g_ref[...]).astype(_BF16)
                          for r in range(0, tm, BLOCK)], axis=0)

    c_kv = c_pool + c_v - c_q
    kv = jnp.dot(ub, w_in_ref[:, c_pool:c_kv], preferred_element_type=_F32)
    q_t[...] = (lax.dot_general(wq_t[...], ub, (((1,), (1,)), ((), ())), preferred_element_type=_F32)
                * (1.0 / float(np.sqrt(HEAD_DIM)))).astype(_BF16)
    k_ext[BLOCK:BLOCK + tm, :] = kv[:, 0:LANES].astype(_BF16)
    vt_ext[:, BLOCK:BLOCK + tm] = kv[:, LANES:2 * LANES].T.astype(_BF16)

    n_heads = N_KV_HEADS * GROUP
    krow = lax.broadcasted_iota(jnp.int32, (2 * BLOCK, n_heads * BLOCK), 0)
    pad_keys = jnp.logical_and(krow < BLOCK, i == 0)
    sinks = jnp.concatenate([jnp.full((1, BLOCK), sinks_ref[head], _F32) for head in range(n_heads)], axis=1)
    no_dims = jnp.zeros((HEAD_DIM, GROUP * BLOCK), _BF16)

    def scores(j):
        keys = k_ext[j * BLOCK:(j + 2) * BLOCK, :]
        rhs = jnp.concatenate([
            jnp.concatenate([
                jnp.concatenate([q_t[(h * GROUP + g) * HEAD_DIM:(h * GROUP + g + 1) * HEAD_DIM,
                                     j * BLOCK:(j + 1) * BLOCK] for g in range(GROUP)], axis=1)
                if hh == h else no_dims for hh in range(N_KV_HEADS)], axis=1)
            for h in range(N_KV_HEADS)], axis=0)
        return jnp.dot(keys, rhs, preferred_element_type=_F32)

    def probs(j, s):
        s = s + bias_ref[...]
        if j == 0:
            s = jnp.where(pad_keys, NEG_INF, s)
        m = jnp.maximum(jnp.max(s, axis=0, keepdims=True), sinks)
        p = jnp.exp(s - m)
        denom = jnp.sum(p, axis=0, keepdims=True) + jnp.exp(sinks - m)
        return p.astype(_BF16), 1.0 / denom

    def weighted_values(j, p, inv):
        r0 = j * BLOCK
        for h in range(N_KV_HEADS):
            lanes = slice(h * GROUP * BLOCK, (h + 1) * GROUP * BLOCK)
            vals = vt_ext[h * HEAD_DIM:(h + 1) * HEAD_DIM, r0:r0 + 2 * BLOCK]
            ot = (jnp.dot(vals, p[:, lanes], preferred_element_type=_F32) * inv[:, lanes]).astype(_BF16)
            for g in range(GROUP):
                d0 = (h * GROUP + g) * HEAD_DIM
                attn_t[d0:d0 + HEAD_DIM, r0:r0 + BLOCK] = ot[:, g * BLOCK:(g + 1) * BLOCK]

    def gate_piece(which, r0, nr, c0, nc):
        col = c_kv + which * d_model + c0
        z = jnp.dot(ub[r0:r0 + nr], w_in_ref[:, col:col + nc], preferred_element_type=_F32)
        gates[which, r0:r0 + nr, c0:c0 + nc] = jax.nn.sigmoid(z)

    steps = list(range(n_blocks))
    piece_cols = 2 * LANES
    pieces = [(which, 0, tm, c0, piece_cols) for which in range(2) for c0 in range(0, d_model, piece_cols)]
    lookahead = 2
    pending = [scores(j) for j in steps[:lookahead]]

    rows = tm + POOL_HALO
    pool_hist[POOL_HALO:rows, :] = jnp.dot(ub, w_in_ref[:, 0:c_pool], preferred_element_type=_F32)

    for n, j in enumerate(steps):
        p, inv = probs(j, pending.pop(0))
        if n + lookahead < len(steps):
            pending.append(scores(steps[n + lookahead]))
        for piece in pieces[n * len(pieces) // len(steps):(n + 1) * len(pieces) // len(steps)]:
            gate_piece(*piece)
        weighted_values(j, p, inv)
    k_ext[0:BLOCK, :] = k_ext[tm:tm + BLOCK, :]
    vt_ext[:, 0:BLOCK] = vt_ext[:, tm:tm + BLOCK]

    y_attn = lax.dot_general(attn_t[...], p_attn_ref[...], (((0,), (0,)), ((), ())),
                             preferred_element_type=_F32)

    w_up_o_ref[...] = w_up_ref[...].astype(_BF16)
    w_down_o_ref[...] = w_down_ref[...].astype(_BF16)

    assert POOL_WINDOWS == tuple(2 ** (k + 1) for k in range(n_groups)) and POOL_HALO == 8 * n_groups
    src, lane0, group_sums = pool_hist, 0, []
    for k, dst in enumerate((pool_s2, pool_s4, pool_s8, None)):
        first, shift = 8 * (k + 1), 2 ** k
        window = src[first:rows, lane0:] + src[first - shift:rows - shift, lane0:]
        if dst is None:
            group_sums.append(window)
        else:
            dst[first:rows, :] = window
            group_sums.append(dst[POOL_HALO:rows, 0:POOL_GROUP_DIM])
            src, lane0 = dst, POOL_GROUP_DIM
    wsum = jnp.concatenate(group_sums, axis=1)
    t1 = i * tm + lax.broadcasted_iota(jnp.int32, (tm, c_pool), 0) + 1
    win = jnp.left_shift(2, lax.broadcasted_iota(jnp.int32, (tm, c_pool), 1) // POOL_GROUP_DIM)
    count = jnp.minimum(t1, win).astype(_F32)
    d = (wsum / count - pool_hist[POOL_HALO:rows, :]).astype(_BF16)
    pool_hist[0:POOL_HALO, :] = pool_hist[tm:rows, :]

    y = jnp.concatenate([jnp.dot(d[:, c * pair:(c + 1) * pair], pool_bd[c], preferred_element_type=_F32)
                         for c in range(n_groups // 2)], axis=1)
    y = ((y + pool_b_ref[...]) * pool_scale_ref[...]).astype(_BF16)
    y_pool = jnp.dot(y, p_pool_ref[...], preferred_element_type=_F32)
    nr = tm // 4
    for r0 in range(0, tm, nr):
        mixed = gates[0, r0:r0 + nr] * y_pool[r0:r0 + nr] + gates[1, r0:r0 + nr] * y_attn[r0:r0 + nr]
        o_ref[r0:r0 + nr, :] = x_ref[r0:r0 + nr, :] + jnp.dot(
            mixed.astype(_BF16), w_out_ref[...], preferred_element_type=_F32)


def _mlp_kernel(h_ref, h_next_ref, g_ref, w_up_ref, w_down_ref, gf_ref, o_ref, ub_scr, *, ff_chunk, final_norm):
    d_ff = w_up_ref.shape[1]
    tm = h_ref.shape[0]

    def normalise_into_scratch(src_ref):
        for r in range(0, tm, BLOCK):
            ub_scr[r:r + BLOCK, :] = _rms_norm(src_ref[r:r + BLOCK, :], g_ref[...]).astype(_BF16)

    @pl.when(pl.program_id(0) == 0)
    def _():
        normalise_into_scratch(h_ref)

    ub = ub_scr[...]

    def up(c):
        a = jnp.maximum(jnp.dot(ub, w_up_ref[:, c:c + ff_chunk], preferred_element_type=_F32), 0.0)
        return (a * a).astype(_BF16)

    chunks = list(range(0, d_ff, ff_chunk))
    acc = h_ref[...]
    act = up(chunks[0])
    for n, c in enumerate(chunks[:-1]):
        nxt = up(chunks[n + 1])
        acc = acc + jnp.dot(act, w_down_ref[c:c + ff_chunk, :], preferred_element_type=_F32)
        act = nxt
        if n == 0:
            normalise_into_scratch(h_next_ref)
    nr = tm // 4
    for r0 in range(0, tm, nr):
        out = acc[r0:r0 + nr] + jnp.dot(act[r0:r0 + nr], w_down_ref[chunks[-1]:chunks[-1] + ff_chunk, :],
                                        preferred_element_type=_F32)
        if final_norm:
            out = _rms_norm(out, gf_ref[...])
        o_ref[r0:r0 + nr, :] = out


def _resident(shape):
    zeros = (0,) * len(shape)
    return pl.BlockSpec(shape, lambda *_: zeros, pipeline_mode=pl.Buffered(1))


def _mixer(x, sinks, g, w_in, pool_w, pool_b, pool_scale, bias, p_pool, p_attn, w_out, w_up, w_down):
    b, s, d = x.shape
    tm = MIXER_TILE
    assert s % tm == 0 and tm % BLOCK == 0
    n_i = s // tm
    d_ff = w_up.shape[1]
    assert d_ff % (b * n_i * LANES) == 0
    ff_slice = d_ff // (b * n_i)
    c_pool = len(POOL_WINDOWS) * POOL_GROUP_DIM
    attn_width = N_KV_HEADS * GROUP * HEAD_DIM
    grid_spec = pltpu.PrefetchScalarGridSpec(
        num_scalar_prefetch=1,
        grid=(b, n_i),
        in_specs=[
            pl.BlockSpec((None, tm, d), lambda bi, i, _: (bi, i, 0)),
            _resident(g.shape), pl.BlockSpec(memory_space=pl.ANY), _resident(pool_w.shape),
            _resident(pool_b.shape), _resident(pool_scale.shape), _resident(bias.shape),
            pl.BlockSpec(memory_space=pl.ANY), pl.BlockSpec(memory_space=pl.ANY), pl.BlockSpec(memory_space=pl.ANY),
            pl.BlockSpec((d, ff_slice), lambda bi, i, _: (0, bi * n_i + i)),
            pl.BlockSpec((ff_slice, d), lambda bi, i, _: (bi * n_i + i, 0)),
        ],
        out_specs=[
            pl.BlockSpec((None, tm, d), lambda bi, i, _: (bi, i, 0)),
            pl.BlockSpec((d, ff_slice), lambda bi, i, _: (0, bi * n_i + i)),
            pl.BlockSpec((ff_slice, d), lambda bi, i, _: (bi * n_i + i, 0)),
        ],
        scratch_shapes=[
            pltpu.VMEM((d, w_in.shape[1] - attn_width), _BF16), pltpu.VMEM(p_pool.shape, _BF16),
            pltpu.VMEM(p_attn.shape, _BF16), pltpu.VMEM(w_out.shape, _BF16),
            pltpu.VMEM((WEIGHT_STAGE_SLOTS,) + WEIGHT_STAGE_BLOCK, _F32),
            pltpu.SemaphoreType.DMA((WEIGHT_STAGE_SLOTS,)),
            pltpu.VMEM((tm + POOL_HALO, c_pool), _F32),
            pltpu.VMEM((tm + POOL_HALO, c_pool), _F32),
            pltpu.VMEM((tm + POOL_HALO, c_pool - POOL_GROUP_DIM), _F32),
            pltpu.VMEM((tm + POOL_HALO, c_pool - 2 * POOL_GROUP_DIM), _F32),
            pltpu.VMEM((len(POOL_WINDOWS) // 2, 2 * POOL_GROUP_DIM, 2 * POOL_GROUP_DIM), _BF16),
            pltpu.VMEM((attn_width, d), _BF16),
            pltpu.VMEM((tm + BLOCK, N_KV_HEADS * HEAD_DIM), _BF16),
            pltpu.VMEM((N_KV_HEADS * HEAD_DIM, tm + BLOCK), _BF16),
            pltpu.VMEM((attn_width, tm), _BF16),
            pltpu.VMEM((2, tm, d), _F32),
            pltpu.VMEM((attn_width, tm), _BF16),
        ],
    )
    return pl.pallas_call(
        functools.partial(_mixer_kernel, tm=tm),
        grid_spec=grid_spec,
        out_shape=[jax.ShapeDtypeStruct(x.shape, _F32),
                   jax.ShapeDtypeStruct(w_up.shape, _BF16), jax.ShapeDtypeStruct(w_down.shape, _BF16)],
        compiler_params=pltpu.CompilerParams(dimension_semantics=("arbitrary", "arbitrary"),
                                             vmem_limit_bytes=MIXER_VMEM_BYTES),
        name="mixer",
    )(sinks, x, g, w_in, pool_w, pool_b, pool_scale, bias, p_pool, p_attn, w_out, w_up, w_down)


def _mlp(h, g, w_up, w_down, g_final, final_norm):
    t, d = h.shape
    tm = MLP_TILE
    assert t % tm == 0
    return pl.pallas_call(
        functools.partial(_mlp_kernel, ff_chunk=1024, final_norm=final_norm),
        grid=(t // tm,),
        in_specs=[
            pl.BlockSpec((tm, d), lambda i: (i, 0)),
            pl.BlockSpec((tm, d), lambda i: (jnp.minimum(i + 1, t // tm - 1), 0)),
            _resident(g.shape), _resident(w_up.shape), _resident(w_down.shape), _resident(g_final.shape),
        ],
        out_specs=pl.BlockSpec((tm, d), lambda i: (i, 0)),
        out_shape=jax.ShapeDtypeStruct(h.shape, _F32),
        scratch_shapes=[pltpu.VMEM((tm, d), _BF16)],
        compiler_params=pltpu.CompilerParams(dimension_semantics=("arbitrary",), vmem_limit_bytes=MLP_VMEM_BYTES),
        name="mlp",
    )(h, h, g, w_up, w_down, g_final)


def kernel(x, norm_mix, w_in, pool_w, pool_b, pool_scale, attn_sinks, p_pool, p_attn,
           w_out, norm_mlp, w_up, w_down, norm_final):
    depth = norm_mix.shape[0]
    b, s, d = x.shape
    bias = jnp.asarray(_attention_bias())
    h = x
    for l in range(depth):
        h, w_up_b, w_down_b = _mixer(
            h, attn_sinks[l], norm_mix[l][None, :], w_in[l], pool_w[l], pool_b[l].reshape(1, -1),
            pool_scale[l][None, :], bias, p_pool[l], p_attn[l], w_out[l], w_up[l], w_down[l])
        h = _mlp(h.reshape(b * s, d), norm_mlp[l][None, :], w_up_b, w_down_b,
                 norm_final[None, :], l == depth - 1).reshape(b, s, d)
    return h
```

```python
import functools

import jax
import jax.numpy as jnp
import numpy as np
from jax import lax
from jax.experimental import pallas as pl
from jax.experimental.pallas import tpu as pltpu

HEAD_DIM = 64
N_KV_HEADS = 2
GROUP = 4
BLOCK = 128
POOL_WINDOWS = (2, 4, 8, 16)
POOL_GROUP_DIM = 128
POOL_HALO = 32
RMS_EPS = 1e-5
NEG_INF = -1e30
LANES = 128

MIXER_TILE = 512
WEIGHT_STAGE_BLOCK = (128, 1024)
WEIGHT_STAGE_SLOTS = 4
MLP_TILE = 1024
MIXER_VMEM_BYTES = 60000 * 1024
MLP_VMEM_BYTES = 48 * 1024 * 1024

_F32 = jnp.float32
_BF16 = jnp.bfloat16


def _rms_norm(x, g):
    return x * lax.rsqrt(jnp.mean(x * x, axis=-1, keepdims=True) + RMS_EPS) * g


def _attention_bias():
    kj = np.arange(2 * BLOCK)[:, None]
    qi = np.arange(BLOCK)[None, :]
    dist = (BLOCK + qi - kj).astype(np.float32)
    valid = (dist >= 0) & (dist < BLOCK)
    n_heads = N_KV_HEADS * GROUP
    out = np.empty((2 * BLOCK, n_heads * BLOCK), np.float32)
    for head in range(n_heads):
        slope = np.float32(2.0 ** (-8.0 * (head + 1) / n_heads))
        out[:, head * BLOCK:(head + 1) * BLOCK] = np.where(valid, -(slope * dist), np.float32(NEG_INF))
    return out


def _mixer_kernel(sinks_ref, x_ref, g_ref, w_in_hbm, pool_w_ref, pool_b_ref, pool_scale_ref,
                  bias_ref, p_pool_hbm, p_attn_hbm, w_out_hbm, w_up_hbm, w_down_hbm, g_mlp_ref, g_final_ref,
                  o_ref,
                  w_in_ref, p_pool_ref, p_attn_ref, w_out_ref, w_up_ref, w_down_ref, stage, stage_sem,
                  pool_hist, pool_s2, pool_s4, pool_s8, pool_bd, wq_t, k_ext, vt_ext, q_t, gates, attn_t,
                  *, tm, final_norm):
    i = pl.program_id(1)
    n_groups = len(POOL_WINDOWS)
    c_pool = n_groups * POOL_GROUP_DIM
    c_q = c_pool + N_KV_HEADS * GROUP * HEAD_DIM
    c_v = c_q + 2 * N_KV_HEADS * HEAD_DIM
    d_model = x_ref.shape[-1]
    n_blocks = tm // BLOCK
    pair = 2 * POOL_GROUP_DIM

    @pl.when(i == 0)
    def _():
        pool_hist[0:POOL_HALO, :] = jnp.zeros((POOL_HALO, c_pool), _F32)
        k_ext[0:BLOCK, :] = jnp.zeros((BLOCK, LANES), _BF16)
        vt_ext[:, 0:BLOCK] = jnp.zeros((LANES, BLOCK), _BF16)

    @pl.when(jnp.logical_and(pl.program_id(0) == 0, i == 0))
    def _():
        srows, scols = stage.shape[1:]
        assert c_q == scols
        chunks = []
        for src, dst in ((w_in_hbm, w_in_ref), (p_pool_hbm, p_pool_ref), (p_attn_hbm, p_attn_ref),
                         (w_out_hbm, w_out_ref), (w_up_hbm, w_up_ref), (w_down_hbm, w_down_ref)):
            for r in range(0, src.shape[0], srows):
                for c in range(0, src.shape[1], scols):
                    w = min(scols, src.shape[1] - c)
                    has_q = src is w_in_hbm and c == 0
                    keep = c_pool if has_q else w
                    c_dst = c - (c_q - c_pool) if src is w_in_hbm and c > 0 else c
                    chunks.append((src.at[r:r + srows, c:c + w], dst.at[r:r + srows, c_dst:c_dst + keep], r, w,
                                   keep, has_q))

        def chunk_copy(n):
            slot = n % stage.shape[0]
            return pltpu.make_async_copy(chunks[n][0], stage.at[slot, :, 0:chunks[n][3]], stage_sem.at[slot])

        for n in range(min(stage.shape[0], len(chunks))):
            chunk_copy(n).start()
        for n, (_, dst, r, w, keep, has_q) in enumerate(chunks):
            chunk_copy(n).wait()
            vals = stage[n % stage.shape[0], :, 0:w]
            dst[...] = vals[:, 0:keep].astype(_BF16)
            if has_q:
                wq_t[:, r:r + srows] = vals[:, c_pool:c_q].T.astype(_BF16)
            if n + stage.shape[0] < len(chunks):
                chunk_copy(n + stage.shape[0]).start()
        pool_bd[...] = jnp.zeros(pool_bd.shape, _BF16)
        for g in range(n_groups):
            lo = (g % 2) * POOL_GROUP_DIM
            pool_bd[g // 2, lo:lo + POOL_GROUP_DIM, lo:lo + POOL_GROUP_DIM] = pool_w_ref[g].astype(_BF16)

    ub = jnp.concatenate([_rms_norm(x_ref[r:r + BLOCK, :], g_ref[...]).astype(_BF16)
                          for r in range(0, tm, BLOCK)], axis=0)

    c_kv = c_pool + c_v - c_q
    kv = jnp.dot(ub, w_in_ref[:, c_pool:c_kv], preferred_element_type=_F32)
    q_t[...] = (lax.dot_general(wq_t[...], ub, (((1,), (1,)), ((), ())), preferred_element_type=_F32)
                * (1.0 / float(np.sqrt(HEAD_DIM)))).astype(_BF16)
    k_ext[BLOCK:BLOCK + tm, :] = kv[:, 0:LANES].astype(_BF16)
    vt_ext[:, BLOCK:BLOCK + tm] = kv[:, LANES:2 * LANES].T.astype(_BF16)

    n_heads = N_KV_HEADS * GROUP
    krow = lax.broadcasted_iota(jnp.int32, (2 * BLOCK, n_heads * BLOCK), 0)
    pad_keys = jnp.logical_and(krow < BLOCK, i == 0)
    sinks = jnp.concatenate([jnp.full((1, BLOCK), sinks_ref[head], _F32) for head in range(n_heads)], axis=1)
    no_dims = jnp.zeros((HEAD_DIM, GROUP * BLOCK), _BF16)

    def scores(j):
        keys = k_ext[j * BLOCK:(j + 2) * BLOCK, :]
        rhs = jnp.concatenate([
            jnp.concatenate([
                jnp.concatenate([q_t[(h * GROUP + g) * HEAD_DIM:(h * GROUP + g + 1) * HEAD_DIM,
                                     j * BLOCK:(j + 1) * BLOCK] for g in range(GROUP)], axis=1)
                if hh == h else no_dims for hh in range(N_KV_HEADS)], axis=1)
            for h in range(N_KV_HEADS)], axis=0)
        return jnp.dot(keys, rhs, preferred_element_type=_F32)

    def probs(j, s):
        s = s + bias_ref[...]
        if j == 0:
            s = jnp.where(pad_keys, NEG_INF, s)
        m = jnp.maximum(jnp.max(s, axis=0, keepdims=True), sinks)
        p = jnp.exp(s - m)
        denom = jnp.sum(p, axis=0, keepdims=True) + jnp.exp(sinks - m)
        return p.astype(_BF16), 1.0 / denom

    def weighted_values(j, p, inv):
        r0 = j * BLOCK
        for h in range(N_KV_HEADS):
            lanes = slice(h * GROUP * BLOCK, (h + 1) * GROUP * BLOCK)
            vals = vt_ext[h * HEAD_DIM:(h + 1) * HEAD_DIM, r0:r0 + 2 * BLOCK]
            ot = (jnp.dot(vals, p[:, lanes], preferred_element_type=_F32) * inv[:, lanes]).astype(_BF16)
            for g in range(GROUP):
                d0 = (h * GROUP + g) * HEAD_DIM
                attn_t[d0:d0 + HEAD_DIM, r0:r0 + BLOCK] = ot[:, g * BLOCK:(g + 1) * BLOCK]

    def gate_piece(which, r0, nr, c0, nc):
        col = c_kv + which * d_model + c0
        z = jnp.dot(ub[r0:r0 + nr], w_in_ref[:, col:col + nc], preferred_element_type=_F32)
        gates[which, r0:r0 + nr, c0:c0 + nc] = jax.nn.sigmoid(z)

    steps = list(range(n_blocks))
    piece_cols = 2 * LANES
    pieces = [(which, 0, tm, c0, piece_cols) for which in range(2) for c0 in range(0, d_model, piece_cols)]
    lookahead = 2
    pending = [scores(j) for j in steps[:lookahead]]

    rows = tm + POOL_HALO
    pool_hist[POOL_HALO:rows, :] = jnp.dot(ub, w_in_ref[:, 0:c_pool], preferred_element_type=_F32)

    for n, j in enumerate(steps):
        p, inv = probs(j, pending.pop(0))
        if n + lookahead < len(steps):
            pending.append(scores(steps[n + lookahead]))
        for piece in pieces[n * len(pieces) // len(steps):(n + 1) * len(pieces) // len(steps)]:
            gate_piece(*piece)
        weighted_values(j, p, inv)
    k_ext[0:BLOCK, :] = k_ext[tm:tm + BLOCK, :]
    vt_ext[:, 0:BLOCK] = vt_ext[:, tm:tm + BLOCK]

    y_attn = lax.dot_general(attn_t[...], p_attn_ref[...], (((0,), (0,)), ((), ())),
                             preferred_element_type=_F32)

    assert POOL_WINDOWS == tuple(2 ** (k + 1) for k in range(n_groups)) and POOL_HALO == 8 * n_groups
    src, lane0, group_sums = pool_hist, 0, []
    for k, dst in enumerate((pool_s2, pool_s4, pool_s8, None)):
        first, shift = 8 * (k + 1), 2 ** k
        window = src[first:rows, lane0:] + src[first - shift:rows - shift, lane0:]
        if dst is None:
            group_sums.append(window)
        else:
            dst[first:rows, :] = window
            group_sums.append(dst[POOL_HALO:rows, 0:POOL_GROUP_DIM])
            src, lane0 = dst, POOL_GROUP_DIM
    wsum = jnp.concatenate(group_sums, axis=1)
    t1 = i * tm + lax.broadcasted_iota(jnp.int32, (tm, c_pool), 0) + 1
    win = jnp.left_shift(2, lax.broadcasted_iota(jnp.int32, (tm, c_pool), 1) // POOL_GROUP_DIM)
    count = jnp.minimum(t1, win).astype(_F32)
    d = (wsum / count - pool_hist[POOL_HALO:rows, :]).astype(_BF16)
    pool_hist[0:POOL_HALO, :] = pool_hist[tm:rows, :]

    y = jnp.concatenate([jnp.dot(d[:, c * pair:(c + 1) * pair], pool_bd[c], preferred_element_type=_F32)
                         for c in range(n_groups // 2)], axis=1)
    y = ((y + pool_b_ref[...]) * pool_scale_ref[...]).astype(_BF16)
    y_pool = jnp.dot(y, p_pool_ref[...], preferred_element_type=_F32)
    mixed = (gates[0] * y_pool + gates[1] * y_attn).astype(_BF16)
    h1 = x_ref[...] + jnp.dot(mixed, w_out_ref[...], preferred_element_type=_F32)

    ff_chunk = 1024
    d_ff = w_up_ref.shape[1]
    ub = jnp.concatenate([_rms_norm(h1[r:r + BLOCK], g_mlp_ref[...]).astype(_BF16)
                          for r in range(0, tm, BLOCK)], axis=0)
    gf_ref = g_final_ref

    def up(c):
        a = jnp.maximum(jnp.dot(ub, w_up_ref[:, c:c + ff_chunk], preferred_element_type=_F32), 0.0)
        return (a * a).astype(_BF16)

    chunks = list(range(0, d_ff, ff_chunk))
    acc = h1
    act = up(chunks[0])
    for n, c in enumerate(chunks[:-1]):
        nxt = up(chunks[n + 1])
        acc = acc + jnp.dot(act, w_down_ref[c:c + ff_chunk, :], preferred_element_type=_F32)
        act = nxt
    nr = tm // 4
    for r0 in range(0, tm, nr):
        out = acc[r0:r0 + nr] + jnp.dot(act[r0:r0 + nr], w_down_ref[chunks[-1]:chunks[-1] + ff_chunk, :],
                                        preferred_element_type=_F32)
        if final_norm:
            out = _rms_norm(out, gf_ref[...])
        o_ref[r0:r0 + nr, :] = out


def _resident(shape):
    zeros = (0,) * len(shape)
    return pl.BlockSpec(shape, lambda *_: zeros, pipeline_mode=pl.Buffered(1))


def _block(x, sinks, g, w_in, pool_w, pool_b, pool_scale, bias, p_pool, p_attn, w_out, w_up, w_down,
           g_mlp, g_final, final_norm):
    b, s, d = x.shape
    tm = MIXER_TILE
    assert s % tm == 0 and tm % BLOCK == 0
    n_i = s // tm
    c_pool = len(POOL_WINDOWS) * POOL_GROUP_DIM
    attn_width = N_KV_HEADS * GROUP * HEAD_DIM
    hbm = pl.BlockSpec(memory_space=pl.ANY)
    grid_spec = pltpu.PrefetchScalarGridSpec(
        num_scalar_prefetch=1,
        grid=(b, n_i),
        in_specs=[
            pl.BlockSpec((None, tm, d), lambda bi, i, _: (bi, i, 0)),
            _resident(g.shape), hbm, _resident(pool_w.shape),
            _resident(pool_b.shape), _resident(pool_scale.shape), _resident(bias.shape),
            hbm, hbm, hbm, hbm, hbm, _resident(g_mlp.shape), _resident(g_final.shape),
        ],
        out_specs=pl.BlockSpec((None, tm, d), lambda bi, i, _: (bi, i, 0)),
        scratch_shapes=[
            pltpu.VMEM((d, w_in.shape[1] - attn_width), _BF16), pltpu.VMEM(p_pool.shape, _BF16),
            pltpu.VMEM(p_attn.shape, _BF16), pltpu.VMEM(w_out.shape, _BF16),
            pltpu.VMEM(w_up.shape, _BF16), pltpu.VMEM(w_down.shape, _BF16),
            pltpu.VMEM((WEIGHT_STAGE_SLOTS,) + WEIGHT_STAGE_BLOCK, _F32),
            pltpu.SemaphoreType.DMA((WEIGHT_STAGE_SLOTS,)),
            pltpu.VMEM((tm + POOL_HALO, c_pool), _F32),
            pltpu.VMEM((tm + POOL_HALO, c_pool), _F32),
            pltpu.VMEM((tm + POOL_HALO, c_pool - POOL_GROUP_DIM), _F32),
            pltpu.VMEM((tm + POOL_HALO, c_pool - 2 * POOL_GROUP_DIM), _F32),
            pltpu.VMEM((len(POOL_WINDOWS) // 2, 2 * POOL_GROUP_DIM, 2 * POOL_GROUP_DIM), _BF16),
            pltpu.VMEM((attn_width, d), _BF16),
            pltpu.VMEM((tm + BLOCK, N_KV_HEADS * HEAD_DIM), _BF16),
            pltpu.VMEM((N_KV_HEADS * HEAD_DIM, tm + BLOCK), _BF16),
            pltpu.VMEM((attn_width, tm), _BF16),
            pltpu.VMEM((2, tm, d), _F32),
            pltpu.VMEM((attn_width, tm), _BF16),
        ],
    )
    return pl.pallas_call(
        functools.partial(_mixer_kernel, tm=tm, final_norm=final_norm),
        grid_spec=grid_spec,
        out_shape=jax.ShapeDtypeStruct(x.shape, _F32),
        compiler_params=pltpu.CompilerParams(dimension_semantics=("arbitrary", "arbitrary"),
                                             vmem_limit_bytes=MIXER_VMEM_BYTES),
        name="block",
    )(sinks, x, g, w_in, pool_w, pool_b, pool_scale, bias, p_pool, p_attn, w_out, w_up, w_down, g_mlp, g_final)


def kernel(x, norm_mix, w_in, pool_w, pool_b, pool_scale, attn_sinks, p_pool, p_attn,
           w_out, norm_mlp, w_up, w_down, norm_final):
    depth = norm_mix.shape[0]
    b, s, d = x.shape
    bias = jnp.asarray(_attention_bias())
    h = x
    for l in range(depth):
        h = _block(h, attn_sinks[l], norm_mix[l][None, :], w_in[l], pool_w[l], pool_b[l].reshape(1, -1),
                   pool_scale[l][None, :], bias, p_pool[l], p_attn[l], w_out[l], w_up[l], w_down[l],
                   norm_mlp[l][None, :], norm_final[None, :], l == depth - 1)
    return h
```

```python
import functools

import jax
import jax.numpy as jnp
import numpy as np
from jax import lax
from jax.experimental import pallas as pl
from jax.experimental.pallas import tpu as pltpu

HEAD_DIM = 64
N_KV_HEADS = 2
GROUP = 4
BLOCK = 128
POOL_WINDOWS = (2, 4, 8, 16)
POOL_GROUP_DIM = 128
POOL_HALO = 32
RMS_EPS = 1e-5
NEG_INF = -1e30
LANES = 128

MIXER_TILE = 1024
WEIGHT_STAGE_BLOCK = (128, 1024)
WEIGHT_STAGE_SLOTS = 8
MLP_TILE = 1024
MIXER_VMEM_BYTES = 60000 * 1024
MLP_VMEM_BYTES = 48 * 1024 * 1024

_F32 = jnp.float32
_BF16 = jnp.bfloat16


def _rms_norm(x, g):
    return x * lax.rsqrt(jnp.mean(x * x, axis=-1, keepdims=True) + RMS_EPS) * g


def _attention_bias():
    kj = np.arange(2 * BLOCK)[:, None]
    qi = np.arange(BLOCK)[None, :]
    dist = (BLOCK + qi - kj).astype(np.float32)
    valid = (dist >= 0) & (dist < BLOCK)
    n_heads = N_KV_HEADS * GROUP
    out = np.empty((2 * BLOCK, n_heads * BLOCK), np.float32)
    for head in range(n_heads):
        slope = np.float32(2.0 ** (-8.0 * (head + 1) / n_heads))
        out[:, head * BLOCK:(head + 1) * BLOCK] = np.where(valid, -(slope * dist), np.float32(NEG_INF))
    return out


def _mixer_kernel(sinks_ref, x_ref, g_ref, w_in_hbm, pool_w_ref, pool_b_ref, pool_scale_ref,
                  bias_ref, p_pool_hbm, p_attn_hbm, w_out_hbm, w_up_ref, w_down_ref,
                  o_ref, w_up_o_ref, w_down_o_ref,
                  w_in_ref, p_pool_ref, p_attn_ref, w_out_ref, stage, stage_sem,
                  pool_hist, pool_s2, pool_s4, pool_s8, pool_bd, wq_t, k_ext, vt_ext, q_t, gates, attn_t, *, tm, n_seq):
    step = pl.program_id(0)
    i = step % (pl.num_programs(0) // n_seq)
    n_groups = len(POOL_WINDOWS)
    c_pool = n_groups * POOL_GROUP_DIM
    c_q = c_pool + N_KV_HEADS * GROUP * HEAD_DIM
    c_v = c_q + 2 * N_KV_HEADS * HEAD_DIM
    d_model = x_ref.shape[-1]
    n_blocks = tm // BLOCK
    pair = 2 * POOL_GROUP_DIM

    @pl.when(i == 0)
    def _():
        pool_hist[0:POOL_HALO, :] = jnp.zeros((POOL_HALO, c_pool), _F32)
        k_ext[0:BLOCK, :] = jnp.zeros((BLOCK, LANES), _BF16)
        vt_ext[:, 0:BLOCK] = jnp.zeros((LANES, BLOCK), _BF16)

    @pl.when(step == 0)
    def _():
        srows, scols = stage.shape[1:]
        assert c_q == scols
        chunks = []
        for src, dst in ((w_in_hbm, w_in_ref), (p_pool_hbm, p_pool_ref), (p_attn_hbm, p_attn_ref),
                         (w_out_hbm, w_out_ref)):
            for r in range(0, src.shape[0], srows):
                for c in range(0, src.shape[1], scols):
                    w = min(scols, src.shape[1] - c)
                    has_q = src is w_in_hbm and c == 0
                    keep = c_pool if has_q else w
                    c_dst = c - (c_q - c_pool) if src is w_in_hbm and c > 0 else c
                    chunks.append((src.at[r:r + srows, c:c + w], dst.at[r:r + srows, c_dst:c_dst + keep], r, w,
                                   keep, has_q))

        def chunk_copy(n):
            slot = n % stage.shape[0]
            return pltpu.make_async_copy(chunks[n][0], stage.at[slot, :, 0:chunks[n][3]], stage_sem.at[slot])

        for n in range(min(stage.shape[0], len(chunks))):
            chunk_copy(n).start()
        for n, (_, dst, r, w, keep, has_q) in enumerate(chunks):
            chunk_copy(n).wait()
            vals = stage[n % stage.shape[0], :, 0:w]
            dst[...] = vals[:, 0:keep].astype(_BF16)
            if has_q:
                wq_t[:, r:r + srows] = vals[:, c_pool:c_q].T.astype(_BF16)
            if n + stage.shape[0] < len(chunks):
                chunk_copy(n + stage.shape[0]).start()
        pool_bd[...] = jnp.zeros(pool_bd.shape, _BF16)
        for g in range(n_groups):
            lo = (g % 2) * POOL_GROUP_DIM
            pool_bd[g // 2, lo:lo + POOL_GROUP_DIM, lo:lo + POOL_GROUP_DIM] = pool_w_ref[g].astype(_BF16)

    ub = jnp.concatenate([_rms_norm(x_ref[r:r + BLOCK, :], g_ref[...]).astype(_BF16)
                          for r in range(0, tm, BLOCK)], axis=0)

    c_kv = c_pool + c_v - c_q
    kv = jnp.dot(ub, w_in_ref[:, c_pool:c_kv], preferred_element_type=_F32)
    q_t[...] = (lax.dot_general(wq_t[...], ub, (((1,), (1,)), ((), ())), preferred_element_type=_F32)
                * (1.0 / float(np.sqrt(HEAD_DIM)))).astype(_BF16)
    k_ext[BLOCK:BLOCK + tm, :] = kv[:, 0:LANES].astype(_BF16)
    vt_ext[:, BLOCK:BLOCK + tm] = kv[:, LANES:2 * LANES].T.astype(_BF16)

    n_heads = N_KV_HEADS * GROUP
    krow = lax.broadcasted_iota(jnp.int32, (2 * BLOCK, n_heads * BLOCK), 0)
    pad_keys = jnp.logical_and(krow < BLOCK, i == 0)
    sinks = jnp.concatenate([jnp.full((1, BLOCK), sinks_ref[head], _F32) for head in range(n_heads)], axis=1)
    no_dims = jnp.zeros((HEAD_DIM, GROUP * BLOCK), _BF16)

    def scores(j):
        keys = k_ext[j * BLOCK:(j + 2) * BLOCK, :]
        rhs = jnp.concatenate([
            jnp.concatenate([
                jnp.concatenate([q_t[(h * GROUP + g) * HEAD_DIM:(h * GROUP + g + 1) * HEAD_DIM,
                                     j * BLOCK:(j + 1) * BLOCK] for g in range(GROUP)], axis=1)
                if hh == h else no_dims for hh in range(N_KV_HEADS)], axis=1)
            for h in range(N_KV_HEADS)], axis=0)
        return jnp.dot(keys, rhs, preferred_element_type=_F32)

    def probs(j, s):
        s = s + bias_ref[...]
        if j == 0:
            s = jnp.where(pad_keys, NEG_INF, s)
        m = jnp.maximum(jnp.max(s, axis=0, keepdims=True), sinks)
        p = jnp.exp(s - m)
        denom = jnp.sum(p, axis=0, keepdims=True) + jnp.exp(sinks - m)
        return p.astype(_BF16), 1.0 / denom

    def weighted_values(j, p, inv):
        r0 = j * BLOCK
        for h in range(N_KV_HEADS):
            lanes = slice(h * GROUP * BLOCK, (h + 1) * GROUP * BLOCK)
            vals = vt_ext[h * HEAD_DIM:(h + 1) * HEAD_DIM, r0:r0 + 2 * BLOCK]
            ot = (jnp.dot(vals, p[:, lanes], preferred_element_type=_F32) * inv[:, lanes]).astype(_BF16)
            for g in range(GROUP):
                d0 = (h * GROUP + g) * HEAD_DIM
                attn_t[d0:d0 + HEAD_DIM, r0:r0 + BLOCK] = ot[:, g * BLOCK:(g + 1) * BLOCK]

    def gate_piece(which, r0, nr, c0, nc):
        col = c_kv + which * d_model + c0
        z = jnp.dot(ub[r0:r0 + nr], w_in_ref[:, col:col + nc], preferred_element_type=_F32)
        gates[which, r0:r0 + nr, c0:c0 + nc] = jax.nn.sigmoid(z)

    steps = list(range(n_blocks))
    piece_cols = 2 * LANES
    pieces = [(which, 0, tm, c0, piece_cols) for which in range(2) for c0 in range(0, d_model, piece_cols)]
    lookahead = 2
    pending = [scores(j) for j in steps[:lookahead]]

    rows = tm + POOL_HALO
    pool_hist[POOL_HALO:rows, :] = jnp.dot(ub, w_in_ref[:, 0:c_pool], preferred_element_type=_F32)

    for n, j in enumerate(steps):
        p, inv = probs(j, pending.pop(0))
        if n + lookahead < len(steps):
            pending.append(scores(steps[n + lookahead]))
        for piece in pieces[n * len(pieces) // len(steps):(n + 1) * len(pieces) // len(steps)]:
            gate_piece(*piece)
        weighted_values(j, p, inv)
    k_ext[0:BLOCK, :] = k_ext[tm:tm + BLOCK, :]
    vt_ext[:, 0:BLOCK] = vt_ext[:, tm:tm + BLOCK]

    y_attn = lax.dot_general(attn_t[...], p_attn_ref[...], (((0,), (0,)), ((), ())),
                             preferred_element_type=_F32)

    w_up_o_ref[...] = w_up_ref[...].astype(_BF16)
    w_down_o_ref[...] = w_down_ref[...].astype(_BF16)

    assert POOL_WINDOWS == tuple(2 ** (k + 1) for k in range(n_groups)) and POOL_HALO == 8 * n_groups
    src, lane0, group_sums = pool_hist, 0, []
    for k, dst in enumerate((pool_s2, pool_s4, pool_s8, None)):
        first, shift = 8 * (k + 1), 2 ** k
        window = src[first:rows, lane0:] + src[first - shift:rows - shift, lane0:]
        if dst is None:
            group_sums.append(window)
        else:
            dst[first:rows, :] = window
            group_sums.append(dst[POOL_HALO:rows, 0:POOL_GROUP_DIM])
            src, lane0 = dst, POOL_GROUP_DIM
    wsum = jnp.concatenate(group_sums, axis=1)
    t1 = i * tm + lax.broadcasted_iota(jnp.int32, (tm, c_pool), 0) + 1
    win = jnp.left_shift(2, lax.broadcasted_iota(jnp.int32, (tm, c_pool), 1) // POOL_GROUP_DIM)
    count = jnp.minimum(t1, win).astype(_F32)
    d = (wsum / count - pool_hist[POOL_HALO:rows, :]).astype(_BF16)
    pool_hist[0:POOL_HALO, :] = pool_hist[tm:rows, :]

    y = jnp.concatenate([jnp.dot(d[:, c * pair:(c + 1) * pair], pool_bd[c], preferred_element_type=_F32)
                         for c in range(n_groups // 2)], axis=1)
    y = ((y + pool_b_ref[...]) * pool_scale_ref[...]).astype(_BF16)
    y_pool = jnp.dot(y, p_pool_ref[...], preferred_element_type=_F32)
    nr = tm // 4
    for r0 in range(0, tm, nr):
        mixed = gates[0, r0:r0 + nr] * y_pool[r0:r0 + nr] + gates[1, r0:r0 + nr] * y_attn[r0:r0 + nr]
        o_ref[r0:r0 + nr, :] = x_ref[r0:r0 + nr, :] + jnp.dot(
            mixed.astype(_BF16), w_out_ref[...], preferred_element_type=_F32)


def _mlp_kernel(h_ref, g_ref, w_up_ref, w_down_ref, gf_ref, o_ref, *, ff_chunk, final_norm):
    d_ff = w_up_ref.shape[1]
    tm = h_ref.shape[0]
    ub = jnp.concatenate([_rms_norm(h_ref[r:r + BLOCK, :], g_ref[...]).astype(_BF16)
                          for r in range(0, tm, BLOCK)], axis=0)
    def up(c):
        a = jnp.maximum(jnp.dot(ub, w_up_ref[:, c:c + ff_chunk], preferred_element_type=_F32), 0.0)
        return (a * a).astype(_BF16)

    chunks = list(range(0, d_ff, ff_chunk))
    acc = h_ref[...]
    act = up(chunks[0])
    for n, c in enumerate(chunks[:-1]):
        nxt = up(chunks[n + 1])
        acc = acc + jnp.dot(act, w_down_ref[c:c + ff_chunk, :], preferred_element_type=_F32)
        act = nxt
    nr = tm // 4
    for r0 in range(0, tm, nr):
        out = acc[r0:r0 + nr] + jnp.dot(act[r0:r0 + nr], w_down_ref[chunks[-1]:chunks[-1] + ff_chunk, :],
                                        preferred_element_type=_F32)
        if final_norm:
            out = _rms_norm(out, gf_ref[...])
        o_ref[r0:r0 + nr, :] = out


def _resident(shape):
    zeros = (0,) * len(shape)
    return pl.BlockSpec(shape, lambda *_: zeros, pipeline_mode=pl.Buffered(1))


def _mixer(x, sinks, g, w_in, pool_w, pool_b, pool_scale, bias, p_pool, p_attn, w_out, w_up, w_down):
    b, s, d = x.shape
    tm = MIXER_TILE
    assert s % tm == 0 and tm % BLOCK == 0
    n_i = s // tm
    d_ff = w_up.shape[1]
    assert d_ff % (b * n_i * LANES) == 0
    ff_slice = d_ff // (b * n_i)
    c_pool = len(POOL_WINDOWS) * POOL_GROUP_DIM
    attn_width = N_KV_HEADS * GROUP * HEAD_DIM
    grid_spec = pltpu.PrefetchScalarGridSpec(
        num_scalar_prefetch=1,
        grid=(b * n_i,),
        in_specs=[
            pl.BlockSpec((None, tm, d), lambda t, _: (t // n_i, t % n_i, 0)),
            _resident(g.shape), pl.BlockSpec(memory_space=pl.ANY), _resident(pool_w.shape),
            _resident(pool_b.shape), _resident(pool_scale.shape), _resident(bias.shape),
            pl.BlockSpec(memory_space=pl.ANY), pl.BlockSpec(memory_space=pl.ANY), pl.BlockSpec(memory_space=pl.ANY),
            pl.BlockSpec((d, ff_slice), lambda t, _: (0, t)),
            pl.BlockSpec((ff_slice, d), lambda t, _: (t, 0)),
        ],
        out_specs=[
            pl.BlockSpec((None, tm, d), lambda t, _: (t // n_i, t % n_i, 0)),
            pl.BlockSpec((d, ff_slice), lambda t, _: (0, t)),
            pl.BlockSpec((ff_slice, d), lambda t, _: (t, 0)),
        ],
        scratch_shapes=[
            pltpu.VMEM((d, w_in.shape[1] - attn_width), _BF16), pltpu.VMEM(p_pool.shape, _BF16),
            pltpu.VMEM(p_attn.shape, _BF16), pltpu.VMEM(w_out.shape, _BF16),
            pltpu.VMEM((WEIGHT_STAGE_SLOTS,) + WEIGHT_STAGE_BLOCK, _F32),
            pltpu.SemaphoreType.DMA((WEIGHT_STAGE_SLOTS,)),
            pltpu.VMEM((tm + POOL_HALO, c_pool), _F32),
            pltpu.VMEM((tm + POOL_HALO, c_pool), _F32),
            pltpu.VMEM((tm + POOL_HALO, c_pool - POOL_GROUP_DIM), _F32),
            pltpu.VMEM((tm + POOL_HALO, c_pool - 2 * POOL_GROUP_DIM), _F32),
            pltpu.VMEM((len(POOL_WINDOWS) // 2, 2 * POOL_GROUP_DIM, 2 * POOL_GROUP_DIM), _BF16),
            pltpu.VMEM((attn_width, d), _BF16),
            pltpu.VMEM((tm + BLOCK, N_KV_HEADS * HEAD_DIM), _BF16),
            pltpu.VMEM((N_KV_HEADS * HEAD_DIM, tm + BLOCK), _BF16),
            pltpu.VMEM((attn_width, tm), _BF16),
            pltpu.VMEM((2, tm, d), _F32),
            pltpu.VMEM((attn_width, tm), _BF16),
        ],
    )
    return pl.pallas_call(
        functools.partial(_mixer_kernel, tm=tm, n_seq=b),
        grid_spec=grid_spec,
        out_shape=[jax.ShapeDtypeStruct(x.shape, _F32),
                   jax.ShapeDtypeStruct(w_up.shape, _BF16), jax.ShapeDtypeStruct(w_down.shape, _BF16)],
        compiler_params=pltpu.CompilerParams(dimension_semantics=("arbitrary",),
                                             vmem_limit_bytes=MIXER_VMEM_BYTES),
        name="mixer",
    )(sinks, x, g, w_in, pool_w, pool_b, pool_scale, bias, p_pool, p_attn, w_out, w_up, w_down)


def _mlp(h, g, w_up, w_down, g_final, final_norm):
    t, d = h.shape
    tm = MLP_TILE
    assert t % tm == 0
    return pl.pallas_call(
        functools.partial(_mlp_kernel, ff_chunk=1024, final_norm=final_norm),
        grid=(t // tm,),
        in_specs=[
            pl.BlockSpec((tm, d), lambda i: (i, 0)),
            _resident(g.shape), _resident(w_up.shape), _resident(w_down.shape), _resident(g_final.shape),
        ],
        out_specs=pl.BlockSpec((tm, d), lambda i: (i, 0)),
        out_shape=jax.ShapeDtypeStruct(h.shape, _F32),
        compiler_params=pltpu.CompilerParams(dimension_semantics=("arbitrary",), vmem_limit_bytes=MLP_VMEM_BYTES),
        name="mlp",
    )(h, g, w_up, w_down, g_final)


def kernel(x, norm_mix, w_in, pool_w, pool_b, pool_scale, attn_sinks, p_pool, p_attn,
           w_out, norm_mlp, w_up, w_down, norm_final):
    depth = norm_mix.shape[0]
    b, s, d = x.shape
    bias = jnp.asarray(_attention_bias())
    h = x
    for l in range(depth):
        h, w_up_b, w_down_b = _mixer(
            h, attn_sinks[l], norm_mix[l][None, :], w_in[l], pool_w[l], pool_b[l].reshape(1, -1),
            pool_scale[l][None, :], bias, p_pool[l], p_attn[l], w_out[l], w_up[l], w_down[l])
        h = _mlp(h.reshape(b * s, d), norm_mlp[l][None, :], w_up_b, w_down_b,
                 norm_final[None, :], l == depth - 1).reshape(b, s, d)
    return h
```

```python
import functools

import jax
import jax.numpy as jnp
import numpy as np
from jax import lax
from jax.experimental import pallas as pl
from jax.experimental.pallas import tpu as pltpu

HEAD_DIM = 64
N_KV_HEADS = 2
GROUP = 4
BLOCK = 128
POOL_WINDOWS = (2, 4, 8, 16)
POOL_GROUP_DIM = 128
POOL_HALO = 32
RMS_EPS = 1e-5
NEG_INF = -1e30
LANES = 128

MIXER_TILE = 1024
WEIGHT_STAGE_BLOCK = (64, 1024)
WEIGHT_STAGE_SLOTS = 16
MLP_TILE = 1024
MIXER_VMEM_BYTES = 60000 * 1024
MLP_VMEM_BYTES = 48 * 1024 * 1024

_F32 = jnp.float32
_BF16 = jnp.bfloat16


def _rms_norm(x, g):
    return x * lax.rsqrt(jnp.mean(x * x, axis=-1, keepdims=True) + RMS_EPS) * g


def _attention_bias():
    kj = np.arange(2 * BLOCK)[:, None]
    qi = np.arange(BLOCK)[None, :]
    dist = (BLOCK + qi - kj).astype(np.float32)
    valid = (dist >= 0) & (dist < BLOCK)
    n_heads = N_KV_HEADS * GROUP
    out = np.empty((2 * BLOCK, n_heads * BLOCK), np.float32)
    for head in range(n_heads):
        slope = np.float32(2.0 ** (-8.0 * (head + 1) / n_heads))
        out[:, head * BLOCK:(head + 1) * BLOCK] = np.where(valid, -(slope * dist), np.float32(NEG_INF))
    return out


def _mixer_kernel(sinks_ref, x_ref, g_ref, w_in_hbm, pool_w_ref, pool_b_ref, pool_scale_ref,
                  bias_ref, p_pool_hbm, p_attn_hbm, w_out_hbm, w_up_ref, w_down_ref,
                  o_ref, w_up_o_ref, w_down_o_ref,
                  w_in_ref, p_pool_ref, p_attn_ref, w_out_ref, stage, stage_sem,
                  pool_hist, pool_s2, pool_s4, pool_s8, pool_bd, wq_t, k_ext, vt_ext, q_t, gates, attn_t, *, tm, n_seq):
    step = pl.program_id(0)
    i = step % (pl.num_programs(0) // n_seq)
    n_groups = len(POOL_WINDOWS)
    c_pool = n_groups * POOL_GROUP_DIM
    c_q = c_pool + N_KV_HEADS * GROUP * HEAD_DIM
    c_v = c_q + 2 * N_KV_HEADS * HEAD_DIM
    d_model = x_ref.shape[-1]
    n_blocks = tm // BLOCK
    pair = 2 * POOL_GROUP_DIM

    @pl.when(i == 0)
    def _():
        pool_hist[0:POOL_HALO, :] = jnp.zeros((POOL_HALO, c_pool), _F32)
        k_ext[0:BLOCK, :] = jnp.zeros((BLOCK, LANES), _BF16)
        vt_ext[:, 0:BLOCK] = jnp.zeros((LANES, BLOCK), _BF16)

    @pl.when(step == 0)
    def _():
        srows, scols = stage.shape[1:]
        assert c_q == scols
        chunks = []
        for src, dst in ((w_in_hbm, w_in_ref), (p_pool_hbm, p_pool_ref), (p_attn_hbm, p_attn_ref),
                         (w_out_hbm, w_out_ref)):
            for r in range(0, src.shape[0], srows):
                for c in range(0, src.shape[1], scols):
                    w = min(scols, src.shape[1] - c)
                    has_q = src is w_in_hbm and c == 0
                    keep = c_pool if has_q else w
                    c_dst = c - (c_q - c_pool) if src is w_in_hbm and c > 0 else c
                    chunks.append((src.at[r:r + srows, c:c + w], dst.at[r:r + srows, c_dst:c_dst + keep], r, w,
                                   keep, has_q))

        def chunk_copy(n):
            slot = n % stage.shape[0]
            return pltpu.make_async_copy(chunks[n][0], stage.at[slot, :, 0:chunks[n][3]], stage_sem.at[slot])

        for n in range(min(stage.shape[0], len(chunks))):
            chunk_copy(n).start()
        for n, (_, dst, r, w, keep, has_q) in enumerate(chunks):
            chunk_copy(n).wait()
            vals = stage[n % stage.shape[0], :, 0:w]
            dst[...] = vals[:, 0:keep].astype(_BF16)
            if has_q:
                wq_t[:, r:r + srows] = vals[:, c_pool:c_q].T.astype(_BF16)
            if n + stage.shape[0] < len(chunks):
                chunk_copy(n + stage.shape[0]).start()
        pool_bd[...] = jnp.zeros(pool_bd.shape, _BF16)
        for g in range(n_groups):
            lo = (g % 2) * POOL_GROUP_DIM
            pool_bd[g // 2, lo:lo + POOL_GROUP_DIM, lo:lo + POOL_GROUP_DIM] = pool_w_ref[g].astype(_BF16)

    ub = jnp.concatenate([_rms_norm(x_ref[r:r + BLOCK, :], g_ref[...]).astype(_BF16)
                          for r in range(0, tm, BLOCK)], axis=0)

    c_kv = c_pool + c_v - c_q
    kv = jnp.dot(ub, w_in_ref[:, c_pool:c_kv], preferred_element_type=_F32)
    q_t[...] = (lax.dot_general(wq_t[...], ub, (((1,), (1,)), ((), ())), preferred_element_type=_F32)
                * (1.0 / float(np.sqrt(HEAD_DIM)))).astype(_BF16)
    k_ext[BLOCK:BLOCK + tm, :] = kv[:, 0:LANES].astype(_BF16)
    vt_ext[:, BLOCK:BLOCK + tm] = kv[:, LANES:2 * LANES].T.astype(_BF16)

    n_heads = N_KV_HEADS * GROUP
    krow = lax.broadcasted_iota(jnp.int32, (2 * BLOCK, n_heads * BLOCK), 0)
    pad_keys = jnp.logical_and(krow < BLOCK, i == 0)
    sinks = jnp.concatenate([jnp.full((1, BLOCK), sinks_ref[head], _F32) for head in range(n_heads)], axis=1)
    no_dims = jnp.zeros((HEAD_DIM, GROUP * BLOCK), _BF16)

    def scores(j):
        keys = k_ext[j * BLOCK:(j + 2) * BLOCK, :]
        rhs = jnp.concatenate([
            jnp.concatenate([
                jnp.concatenate([q_t[(h * GROUP + g) * HEAD_DIM:(h * GROUP + g + 1) * HEAD_DIM,
                                     j * BLOCK:(j + 1) * BLOCK] for g in range(GROUP)], axis=1)
                if hh == h else no_dims for hh in range(N_KV_HEADS)], axis=1)
            for h in range(N_KV_HEADS)], axis=0)
        return jnp.dot(keys, rhs, preferred_element_type=_F32)

    def probs(j, s):
        s = s + bias_ref[...]
        if j == 0:
            s = jnp.where(pad_keys, NEG_INF, s)
        m = jnp.maximum(jnp.max(s, axis=0, keepdims=True), sinks)
        p = jnp.exp(s - m)
        denom = jnp.sum(p, axis=0, keepdims=True) + jnp.exp(sinks - m)
        return p.astype(_BF16), 1.0 / denom

    def weighted_values(j, p, inv):
        r0 = j * BLOCK
        for h in range(N_KV_HEADS):
            lanes = slice(h * GROUP * BLOCK, (h + 1) * GROUP * BLOCK)
            vals = vt_ext[h * HEAD_DIM:(h + 1) * HEAD_DIM, r0:r0 + 2 * BLOCK]
            ot = (jnp.dot(vals, p[:, lanes], preferred_element_type=_F32) * inv[:, lanes]).astype(_BF16)
            for g in range(GROUP):
                d0 = (h * GROUP + g) * HEAD_DIM
                attn_t[d0:d0 + HEAD_DIM, r0:r0 + BLOCK] = ot[:, g * BLOCK:(g + 1) * BLOCK]

    def gate_piece(which, r0, nr, c0, nc):
        col = c_kv + which * d_model + c0
        z = jnp.dot(ub[r0:r0 + nr], w_in_ref[:, col:col + nc], preferred_element_type=_F32)
        gates[which, r0:r0 + nr, c0:c0 + nc] = jax.nn.sigmoid(z)

    steps = list(range(n_blocks))
    piece_cols = 2 * LANES
    pieces = [(which, 0, tm, c0, piece_cols) for which in range(2) for c0 in range(0, d_model, piece_cols)]
    lookahead = 2
    pending = [scores(j) for j in steps[:lookahead]]

    rows = tm + POOL_HALO
    pool_hist[POOL_HALO:rows, :] = jnp.dot(ub, w_in_ref[:, 0:c_pool], preferred_element_type=_F32)

    for n, j in enumerate(steps):
        p, inv = probs(j, pending.pop(0))
        if n + lookahead < len(steps):
            pending.append(scores(steps[n + lookahead]))
        for piece in pieces[n * len(pieces) // len(steps):(n + 1) * len(pieces) // len(steps)]:
            gate_piece(*piece)
        weighted_values(j, p, inv)
    k_ext[0:BLOCK, :] = k_ext[tm:tm + BLOCK, :]
    vt_ext[:, 0:BLOCK] = vt_ext[:, tm:tm + BLOCK]

    y_attn = lax.dot_general(attn_t[...], p_attn_ref[...], (((0,), (0,)), ((), ())),
                             preferred_element_type=_F32)

    w_up_o_ref[...] = w_up_ref[...].astype(_BF16)
    w_down_o_ref[...] = w_down_ref[...].astype(_BF16)

    assert POOL_WINDOWS == tuple(2 ** (k + 1) for k in range(n_groups)) and POOL_HALO == 8 * n_groups
    src, lane0, group_sums = pool_hist, 0, []
    for k, dst in enumerate((pool_s2, pool_s4, pool_s8, None)):
        first, shift = 8 * (k + 1), 2 ** k
        window = src[first:rows, lane0:] + src[first - shift:rows - shift, lane0:]
        if dst is None:
            group_sums.append(window)
        else:
            dst[first:rows, :] = window
            group_sums.append(dst[POOL_HALO:rows, 0:POOL_GROUP_DIM])
            src, lane0 = dst, POOL_GROUP_DIM
    wsum = jnp.concatenate(group_sums, axis=1)
    t1 = i * tm + lax.broadcasted_iota(jnp.int32, (tm, c_pool), 0) + 1
    win = jnp.left_shift(2, lax.broadcasted_iota(jnp.int32, (tm, c_pool), 1) // POOL_GROUP_DIM)
    count = jnp.minimum(t1, win).astype(_F32)
    d = (wsum / count - pool_hist[POOL_HALO:rows, :]).astype(_BF16)
    pool_hist[0:POOL_HALO, :] = pool_hist[tm:rows, :]

    y = jnp.concatenate([jnp.dot(d[:, c * pair:(c + 1) * pair], pool_bd[c], preferred_element_type=_F32)
                         for c in range(n_groups // 2)], axis=1)
    y = ((y + pool_b_ref[...]) * pool_scale_ref[...]).astype(_BF16)
    y_pool = jnp.dot(y, p_pool_ref[...], preferred_element_type=_F32)
    nr = tm // 4
    for r0 in range(0, tm, nr):
        mixed = gates[0, r0:r0 + nr] * y_pool[r0:r0 + nr] + gates[1, r0:r0 + nr] * y_attn[r0:r0 + nr]
        o_ref[r0:r0 + nr, :] = x_ref[r0:r0 + nr, :] + jnp.dot(
            mixed.astype(_BF16), w_out_ref[...], preferred_element_type=_F32)


def _mlp_kernel(h_ref, g_ref, w_up_ref, w_down_ref, gf_ref, o_ref, *, ff_chunk, final_norm):
    d_ff = w_up_ref.shape[1]
    tm = h_ref.shape[0]
    ub = jnp.concatenate([_rms_norm(h_ref[r:r + BLOCK, :], g_ref[...]).astype(_BF16)
                          for r in range(0, tm, BLOCK)], axis=0)
    def up(c):
        a = jnp.maximum(jnp.dot(ub, w_up_ref[:, c:c + ff_chunk], preferred_element_type=_F32), 0.0)
        return (a * a).astype(_BF16)

    chunks = list(range(0, d_ff, ff_chunk))
    acc = h_ref[...]
    act = up(chunks[0])
    for n, c in enumerate(chunks[:-1]):
        nxt = up(chunks[n + 1])
        acc = acc + jnp.dot(act, w_down_ref[c:c + ff_chunk, :], preferred_element_type=_F32)
        act = nxt
    nr = tm // 4
    for r0 in range(0, tm, nr):
        out = acc[r0:r0 + nr] + jnp.dot(act[r0:r0 + nr], w_down_ref[chunks[-1]:chunks[-1] + ff_chunk, :],
                                        preferred_element_type=_F32)
        if final_norm:
            out = _rms_norm(out, gf_ref[...])
        o_ref[r0:r0 + nr, :] = out


def _resident(shape):
    zeros = (0,) * len(shape)
    return pl.BlockSpec(shape, lambda *_: zeros, pipeline_mode=pl.Buffered(1))


def _mixer(x, sinks, g, w_in, pool_w, pool_b, pool_scale, bias, p_pool, p_attn, w_out, w_up, w_down):
    b, s, d = x.shape
    tm = MIXER_TILE
    assert s % tm == 0 and tm % BLOCK == 0
    n_i = s // tm
    d_ff = w_up.shape[1]
    assert d_ff % (b * n_i * LANES) == 0
    ff_slice = d_ff // (b * n_i)
    c_pool = len(POOL_WINDOWS) * POOL_GROUP_DIM
    attn_width = N_KV_HEADS * GROUP * HEAD_DIM
    grid_spec = pltpu.PrefetchScalarGridSpec(
        num_scalar_prefetch=1,
        grid=(b * n_i,),
        in_specs=[
            pl.BlockSpec((None, tm, d), lambda t, _: (t // n_i, t % n_i, 0)),
            _resident(g.shape), pl.BlockSpec(memory_space=pl.ANY), _resident(pool_w.shape),
            _resident(pool_b.shape), _resident(pool_scale.shape), _resident(bias.shape),
            pl.BlockSpec(memory_space=pl.ANY), pl.BlockSpec(memory_space=pl.ANY), pl.BlockSpec(memory_space=pl.ANY),
            pl.BlockSpec((d, ff_slice), lambda t, _: (0, t)),
            pl.BlockSpec((ff_slice, d), lambda t, _: (t, 0)),
        ],
        out_specs=[
            pl.BlockSpec((None, tm, d), lambda t, _: (t // n_i, t % n_i, 0)),
            pl.BlockSpec((d, ff_slice), lambda t, _: (0, t)),
            pl.BlockSpec((ff_slice, d), lambda t, _: (t, 0)),
        ],
        scratch_shapes=[
            pltpu.VMEM((d, w_in.shape[1] - attn_width), _BF16), pltpu.VMEM(p_pool.shape, _BF16),
            pltpu.VMEM(p_attn.shape, _BF16), pltpu.VMEM(w_out.shape, _BF16),
            pltpu.VMEM((WEIGHT_STAGE_SLOTS,) + WEIGHT_STAGE_BLOCK, _F32),
            pltpu.SemaphoreType.DMA((WEIGHT_STAGE_SLOTS,)),
            pltpu.VMEM((tm + POOL_HALO, c_pool), _F32),
            pltpu.VMEM((tm + POOL_HALO, c_pool), _F32),
            pltpu.VMEM((tm + POOL_HALO, c_pool - POOL_GROUP_DIM), _F32),
            pltpu.VMEM((tm + POOL_HALO, c_pool - 2 * POOL_GROUP_DIM), _F32),
            pltpu.VMEM((len(POOL_WINDOWS) // 2, 2 * POOL_GROUP_DIM, 2 * POOL_GROUP_DIM), _BF16),
            pltpu.VMEM((attn_width, d), _BF16),
            pltpu.VMEM((tm + BLOCK, N_KV_HEADS * HEAD_DIM), _BF16),
            pltpu.VMEM((N_KV_HEADS * HEAD_DIM, tm + BLOCK), _BF16),
            pltpu.VMEM((attn_width, tm), _BF16),
            pltpu.VMEM((2, tm, d), _F32),
            pltpu.VMEM((attn_width, tm), _BF16),
        ],
    )
    return pl.pallas_call(
        functools.partial(_mixer_kernel, tm=tm, n_seq=b),
        grid_spec=grid_spec,
        out_shape=[jax.ShapeDtypeStruct(x.shape, _F32),
                   jax.ShapeDtypeStruct(w_up.shape, _BF16), jax.ShapeDtypeStruct(w_down.shape, _BF16)],
        compiler_params=pltpu.CompilerParams(dimension_semantics=("arbitrary",),
                                             vmem_limit_bytes=MIXER_VMEM_BYTES),
        name="mixer",
    )(sinks, x, g, w_in, pool_w, pool_b, pool_scale, bias, p_pool, p_attn, w_out, w_up, w_down)


def _mlp(h, g, w_up, w_down, g_final, final_norm):
    t, d = h.shape
    tm = MLP_TILE
    assert t % tm == 0
    return pl.pallas_call(
        functools.partial(_mlp_kernel, ff_chunk=1024, final_norm=final_norm),
        grid=(t // tm,),
        in_specs=[
            pl.BlockSpec((tm, d), lambda i: (i, 0)),
            _resident(g.shape), _resident(w_up.shape), _resident(w_down.shape), _resident(g_final.shape),
        ],
        out_specs=pl.BlockSpec((tm, d), lambda i: (i, 0)),
        out_shape=jax.ShapeDtypeStruct(h.shape, _F32),
        compiler_params=pltpu.CompilerParams(dimension_semantics=("arbitrary",), vmem_limit_bytes=MLP_VMEM_BYTES),
        name="mlp",
    )(h, g, w_up, w_down, g_final)


def kernel(x, norm_mix, w_in, pool_w, pool_b, pool_scale, attn_sinks, p_pool, p_attn,
           w_out, norm_mlp, w_up, w_down, norm_final):
    depth = norm_mix.shape[0]
    b, s, d = x.shape
    bias = jnp.asarray(_attention_bias())
    h = x
    for l in range(depth):
        h, w_up_b, w_down_b = _mixer(
            h, attn_sinks[l], norm_mix[l][None, :], w_in[l], pool_w[l], pool_b[l].reshape(1, -1),
            pool_scale[l][None, :], bias, p_pool[l], p_attn[l], w_out[l], w_up[l], w_down[l])
        h = _mlp(h.reshape(b * s, d), norm_mlp[l][None, :], w_up_b, w_down_b,
                 norm_final[None, :], l == depth - 1).reshape(b, s, d)
    return h
```

```python
import functools

import jax
import jax.numpy as jnp
import numpy as np
from jax import lax
from jax.experimental import pallas as pl
from jax.experimental.pallas import tpu as pltpu

HEAD_DIM = 64
N_KV_HEADS = 2
GROUP = 4
BLOCK = 128
POOL_WINDOWS = (2, 4, 8, 16)
POOL_GROUP_DIM = 128
POOL_HALO = 32
RMS_EPS = 1e-5
NEG_INF = -1e30
LANES = 128

MIXER_TILE = 1024
WEIGHT_STAGE_BLOCK = (128, 1024)
WEIGHT_STAGE_SLOTS = 8
MLP_TILE = 1024
MIXER_VMEM_BYTES = 60000 * 1024
MLP_VMEM_BYTES = 48 * 1024 * 1024

_F32 = jnp.float32
_BF16 = jnp.bfloat16


def _rms_norm(x, g):
    return x * lax.rsqrt(jnp.mean(x * x, axis=-1, keepdims=True) + RMS_EPS) * g


def _attention_bias():
    kj = np.arange(2 * BLOCK)[:, None]
    qi = np.arange(BLOCK)[None, :]
    dist = (BLOCK + qi - kj).astype(np.float32)
    valid = (dist >= 0) & (dist < BLOCK)
    n_heads = N_KV_HEADS * GROUP
    out = np.empty((2 * BLOCK, n_heads * BLOCK), np.float32)
    for head in range(n_heads):
        slope = np.float32(2.0 ** (-8.0 * (head + 1) / n_heads))
        out[:, head * BLOCK:(head + 1) * BLOCK] = np.where(valid, -(slope * dist), np.float32(NEG_INF))
    return out


def _mixer_kernel(sinks_ref, x_ref, g_ref, w_in_hbm, pool_w_ref, pool_b_ref, pool_scale_ref,
                  bias_ref, p_pool_hbm, p_attn_hbm, w_out_hbm, w_up_ref, w_down_ref,
                  o_ref, w_up_o_ref, w_down_o_ref,
                  w_in_ref, p_pool_ref, p_attn_ref, w_out_ref, stage, stage_sem,
                  pool_hist, pool_s2, pool_s4, pool_s8, pool_bd, wq_t, k_ext, vt_ext, q_t, gates, attn_t, *, tm, n_seq):
    step = pl.program_id(0)
    i = step % (pl.num_programs(0) // n_seq)
    n_groups = len(POOL_WINDOWS)
    c_pool = n_groups * POOL_GROUP_DIM
    c_q = c_pool + N_KV_HEADS * GROUP * HEAD_DIM
    c_v = c_q + 2 * N_KV_HEADS * HEAD_DIM
    d_model = x_ref.shape[-1]
    n_blocks = tm // BLOCK
    pair = 2 * POOL_GROUP_DIM

    @pl.when(i == 0)
    def _():
        pool_hist[0:POOL_HALO, :] = jnp.zeros((POOL_HALO, c_pool), _F32)
        k_ext[0:BLOCK, :] = jnp.zeros((BLOCK, LANES), _BF16)
        vt_ext[:, 0:BLOCK] = jnp.zeros((LANES, BLOCK), _BF16)

    @pl.when(step == 0)
    def _():
        srows, scols = stage.shape[1:]
        assert c_q == scols
        chunks = []
        for src, dst in ((w_in_hbm, w_in_ref), (p_pool_hbm, p_pool_ref), (p_attn_hbm, p_attn_ref),
                         (w_out_hbm, w_out_ref)):
            for r in range(0, src.shape[0], srows):
                for c in range(0, src.shape[1], scols):
                    w = min(scols, src.shape[1] - c)
                    has_q = src is w_in_hbm and c == 0
                    keep = c_pool if has_q else w
                    c_dst = c - (c_q - c_pool) if src is w_in_hbm and c > 0 else c
                    chunks.append((src.at[r:r + srows, c:c + w], dst.at[r:r + srows, c_dst:c_dst + keep], r, w,
                                   keep, has_q))

        def chunk_copy(n):
            slot = n % stage.shape[0]
            return pltpu.make_async_copy(chunks[n][0], stage.at[slot, :, 0:chunks[n][3]], stage_sem.at[slot])

        for n in range(min(stage.shape[0], len(chunks))):
            chunk_copy(n).start()
        for n, (_, dst, r, w, keep, has_q) in enumerate(chunks):
            chunk_copy(n).wait()
            vals = stage[n % stage.shape[0], :, 0:w]
            dst[...] = vals[:, 0:keep].astype(_BF16)
            if has_q:
                wq_t[:, r:r + srows] = vals[:, c_pool:c_q].T.astype(_BF16)
            if n + stage.shape[0] < len(chunks):
                chunk_copy(n + stage.shape[0]).start()
        pool_bd[...] = jnp.zeros(pool_bd.shape, _BF16)
        for g in range(n_groups):
            lo = (g % 2) * POOL_GROUP_DIM
            pool_bd[g // 2, lo:lo + POOL_GROUP_DIM, lo:lo + POOL_GROUP_DIM] = pool_w_ref[g].astype(_BF16)

    ub = jnp.concatenate([_rms_norm(x_ref[r:r + BLOCK, :], g_ref[...]).astype(_BF16)
                          for r in range(0, tm, BLOCK)], axis=0)

    c_kv = c_pool + c_v - c_q
    kv = jnp.dot(ub, w_in_ref[:, c_pool:c_kv], preferred_element_type=_F32)
    q_t[...] = (lax.dot_general(wq_t[...], ub, (((1,), (1,)), ((), ())), preferred_element_type=_F32)
                * (1.0 / float(np.sqrt(HEAD_DIM)))).astype(_BF16)
    k_ext[BLOCK:BLOCK + tm, :] = kv[:, 0:LANES].astype(_BF16)
    vt_ext[:, BLOCK:BLOCK + tm] = kv[:, LANES:2 * LANES].T.astype(_BF16)

    n_heads = N_KV_HEADS * GROUP
    krow = lax.broadcasted_iota(jnp.int32, (2 * BLOCK, n_heads * BLOCK), 0)
    pad_keys = jnp.logical_and(krow < BLOCK, i == 0)
    sinks = jnp.concatenate([jnp.full((1, BLOCK), sinks_ref[head], _F32) for head in range(n_heads)], axis=1)
    no_dims = jnp.zeros((HEAD_DIM, GROUP * BLOCK), _BF16)

    def scores(j):
        keys = k_ext[j * BLOCK:(j + 2) * BLOCK, :]
        rhs = jnp.concatenate([
            jnp.concatenate([
                jnp.concatenate([q_t[(h * GROUP + g) * HEAD_DIM:(h * GROUP + g + 1) * HEAD_DIM,
                                     j * BLOCK:(j + 1) * BLOCK] for g in range(GROUP)], axis=1)
                if hh == h else no_dims for hh in range(N_KV_HEADS)], axis=1)
            for h in range(N_KV_HEADS)], axis=0)
        return jnp.dot(keys, rhs, preferred_element_type=_F32)

    def probs(j, s):
        s = s + bias_ref[...]
        if j == 0:
            s = jnp.where(pad_keys, NEG_INF, s)
        m = jnp.maximum(jnp.max(s, axis=0, keepdims=True), sinks)
        p = jnp.exp(s - m)
        denom = jnp.sum(p, axis=0, keepdims=True) + jnp.exp(sinks - m)
        return p.astype(_BF16), 1.0 / denom

    def weighted_values(j, p, inv):
        r0 = j * BLOCK
        for h in range(N_KV_HEADS):
            lanes = slice(h * GROUP * BLOCK, (h + 1) * GROUP * BLOCK)
            vals = vt_ext[h * HEAD_DIM:(h + 1) * HEAD_DIM, r0:r0 + 2 * BLOCK]
            ot = (jnp.dot(vals, p[:, lanes], preferred_element_type=_F32) * inv[:, lanes]).astype(_BF16)
            for g in range(GROUP):
                d0 = (h * GROUP + g) * HEAD_DIM
                attn_t[d0:d0 + HEAD_DIM, r0:r0 + BLOCK] = ot[:, g * BLOCK:(g + 1) * BLOCK]

    def gate_piece(which, r0, nr, c0, nc):
        col = c_kv + which * d_model + c0
        z = jnp.dot(ub[r0:r0 + nr], w_in_ref[:, col:col + nc], preferred_element_type=_F32)
        gates[which, r0:r0 + nr, c0:c0 + nc] = jax.nn.sigmoid(z)

    steps = list(range(n_blocks))
    piece_cols = 2 * LANES
    pieces = [(which, 0, tm, c0, piece_cols) for which in range(2) for c0 in range(0, d_model, piece_cols)]
    lookahead = 2
    pending = [scores(j) for j in steps[:lookahead]]

    rows = tm + POOL_HALO
    pool_hist[POOL_HALO:rows, :] = jnp.dot(ub, w_in_ref[:, 0:c_pool], preferred_element_type=_F32)

    for n, j in enumerate(steps):
        p, inv = probs(j, pending.pop(0))
        if n + lookahead < len(steps):
            pending.append(scores(steps[n + lookahead]))
        for piece in pieces[n * len(pieces) // len(steps):(n + 1) * len(pieces) // len(steps)]:
            gate_piece(*piece)
        weighted_values(j, p, inv)
    k_ext[0:BLOCK, :] = k_ext[tm:tm + BLOCK, :]
    vt_ext[:, 0:BLOCK] = vt_ext[:, tm:tm + BLOCK]

    y_attn = lax.dot_general(attn_t[...], p_attn_ref[...], (((0,), (0,)), ((), ())),
                             preferred_element_type=_F32)

    w_up_o_ref[...] = w_up_ref[...].astype(_BF16)
    w_down_o_ref[...] = w_down_ref[...].astype(_BF16)

    assert POOL_WINDOWS == tuple(2 ** (k + 1) for k in range(n_groups)) and POOL_HALO == 8 * n_groups
    src, lane0, group_sums = pool_hist, 0, []
    for k, dst in enumerate((pool_s2, pool_s4, pool_s8, None)):
        first, shift = 8 * (k + 1), 2 ** k
        window = src[first:rows, lane0:] + src[first - shift:rows - shift, lane0:]
        if dst is None:
            group_sums.append(window)
        else:
            dst[first:rows, :] = window
            group_sums.append(dst[POOL_HALO:rows, 0:POOL_GROUP_DIM])
            src, lane0 = dst, POOL_GROUP_DIM
    wsum = jnp.concatenate(group_sums, axis=1)
    t1 = i * tm + lax.broadcasted_iota(jnp.int32, (tm, c_pool), 0) + 1
    win = jnp.left_shift(2, lax.broadcasted_iota(jnp.int32, (tm, c_pool), 1) // POOL_GROUP_DIM)
    count = jnp.minimum(t1, win).astype(_F32)
    d = (wsum / count - pool_hist[POOL_HALO:rows, :]).astype(_BF16)
    pool_hist[0:POOL_HALO, :] = pool_hist[tm:rows, :]

    y = jnp.concatenate([jnp.dot(d[:, c * pair:(c + 1) * pair], pool_bd[c], preferred_element_type=_F32)
                         for c in range(n_groups // 2)], axis=1)
    y = ((y + pool_b_ref[...]) * pool_scale_ref[...]).astype(_BF16)
    y_pool = jnp.dot(y, p_pool_ref[...], preferred_element_type=_F32)
    nr = tm // 4
    for r0 in range(0, tm, nr):
        mixed = gates[0, r0:r0 + nr] * y_pool[r0:r0 + nr] + gates[1, r0:r0 + nr] * y_attn[r0:r0 + nr]
        o_ref[r0:r0 + nr, :] = x_ref[r0:r0 + nr, :] + jnp.dot(
            mixed.astype(_BF16), w_out_ref[...], preferred_element_type=_F32)


def _mlp_kernel(h_ref, g_ref, w_up_ref, w_down_ref, gf_ref, o_ref, *, ff_chunk, final_norm):
    d_ff = w_up_ref.shape[1]
    tm = h_ref.shape[0]
    ub = jnp.concatenate([_rms_norm(h_ref[r:r + BLOCK, :], g_ref[...]).astype(_BF16)
                          for r in range(0, tm, BLOCK)], axis=0)
    def up(c):
        a = jnp.maximum(jnp.dot(ub, w_up_ref[:, c:c + ff_chunk], preferred_element_type=_F32), 0.0)
        return (a * a).astype(_BF16)

    chunks = list(range(0, d_ff, ff_chunk))
    acc = h_ref[...]
    act = up(chunks[0])
    for n, c in enumerate(chunks[:-1]):
        nxt = up(chunks[n + 1])
        acc = acc + jnp.dot(act, w_down_ref[c:c + ff_chunk, :], preferred_element_type=_F32)
        act = nxt
    nr = tm // 4
    for r0 in range(0, tm, nr):
        out = acc[r0:r0 + nr] + jnp.dot(act[r0:r0 + nr], w_down_ref[chunks[-1]:chunks[-1] + ff_chunk, :],
                                        preferred_element_type=_F32)
        if final_norm:
            out = _rms_norm(out, gf_ref[...])
        o_ref[r0:r0 + nr, :] = out


def _mlp_streamed(h_hbm, g_ref, w_up_ref, w_down_ref, gf_ref, o_hbm, *, tm, ff_chunk, final_norm):
    d = h_hbm.shape[1]

    def step(h_ref, o_ref):
        _mlp_kernel(h_ref, g_ref, w_up_ref, w_down_ref, gf_ref, o_ref, ff_chunk=ff_chunk, final_norm=final_norm)

    pltpu.emit_pipeline(
        step,
        grid=(h_hbm.shape[0] // tm,),
        in_specs=[pl.BlockSpec((tm, d), lambda i: (i, 0))],
        out_specs=[pl.BlockSpec((tm, d), lambda i: (i, 0))],
    )(h_hbm, o_hbm)


def _resident(shape):
    zeros = (0,) * len(shape)
    return pl.BlockSpec(shape, lambda *_: zeros, pipeline_mode=pl.Buffered(1))


def _mixer(x, sinks, g, w_in, pool_w, pool_b, pool_scale, bias, p_pool, p_attn, w_out, w_up, w_down):
    b, s, d = x.shape
    tm = MIXER_TILE
    assert s % tm == 0 and tm % BLOCK == 0
    n_i = s // tm
    d_ff = w_up.shape[1]
    assert d_ff % (b * n_i * LANES) == 0
    ff_slice = d_ff // (b * n_i)
    c_pool = len(POOL_WINDOWS) * POOL_GROUP_DIM
    attn_width = N_KV_HEADS * GROUP * HEAD_DIM
    grid_spec = pltpu.PrefetchScalarGridSpec(
        num_scalar_prefetch=1,
        grid=(b * n_i,),
        in_specs=[
            pl.BlockSpec((None, tm, d), lambda t, _: (t // n_i, t % n_i, 0)),
            _resident(g.shape), pl.BlockSpec(memory_space=pl.ANY), _resident(pool_w.shape),
            _resident(pool_b.shape), _resident(pool_scale.shape), _resident(bias.shape),
            pl.BlockSpec(memory_space=pl.ANY), pl.BlockSpec(memory_space=pl.ANY), pl.BlockSpec(memory_space=pl.ANY),
            pl.BlockSpec((d, ff_slice), lambda t, _: (0, t)),
            pl.BlockSpec((ff_slice, d), lambda t, _: (t, 0)),
        ],
        out_specs=[
            pl.BlockSpec((None, tm, d), lambda t, _: (t // n_i, t % n_i, 0)),
            pl.BlockSpec((d, ff_slice), lambda t, _: (0, t)),
            pl.BlockSpec((ff_slice, d), lambda t, _: (t, 0)),
        ],
        scratch_shapes=[
            pltpu.VMEM((d, w_in.shape[1] - attn_width), _BF16), pltpu.VMEM(p_pool.shape, _BF16),
            pltpu.VMEM(p_attn.shape, _BF16), pltpu.VMEM(w_out.shape, _BF16),
            pltpu.VMEM((WEIGHT_STAGE_SLOTS,) + WEIGHT_STAGE_BLOCK, _F32),
            pltpu.SemaphoreType.DMA((WEIGHT_STAGE_SLOTS,)),
            pltpu.VMEM((tm + POOL_HALO, c_pool), _F32),
            pltpu.VMEM((tm + POOL_HALO, c_pool), _F32),
            pltpu.VMEM((tm + POOL_HALO, c_pool - POOL_GROUP_DIM), _F32),
            pltpu.VMEM((tm + POOL_HALO, c_pool - 2 * POOL_GROUP_DIM), _F32),
            pltpu.VMEM((len(POOL_WINDOWS) // 2, 2 * POOL_GROUP_DIM, 2 * POOL_GROUP_DIM), _BF16),
            pltpu.VMEM((attn_width, d), _BF16),
            pltpu.VMEM((tm + BLOCK, N_KV_HEADS * HEAD_DIM), _BF16),
            pltpu.VMEM((N_KV_HEADS * HEAD_DIM, tm + BLOCK), _BF16),
            pltpu.VMEM((attn_width, tm), _BF16),
            pltpu.VMEM((2, tm, d), _F32),
            pltpu.VMEM((attn_width, tm), _BF16),
        ],
    )
    return pl.pallas_call(
        functools.partial(_mixer_kernel, tm=tm, n_seq=b),
        grid_spec=grid_spec,
        out_shape=[jax.ShapeDtypeStruct(x.shape, _F32),
                   jax.ShapeDtypeStruct(w_up.shape, _BF16), jax.ShapeDtypeStruct(w_down.shape, _BF16)],
        compiler_params=pltpu.CompilerParams(dimension_semantics=("arbitrary",),
                                             vmem_limit_bytes=MIXER_VMEM_BYTES),
        name="mixer",
    )(sinks, x, g, w_in, pool_w, pool_b, pool_scale, bias, p_pool, p_attn, w_out, w_up, w_down)


def _mlp(h, g, w_up, w_down, g_final, final_norm):
    t, d = h.shape
    tm = MLP_TILE
    assert t % tm == 0
    vmem = pl.BlockSpec(memory_space=pltpu.VMEM)
    return pl.pallas_call(
        functools.partial(_mlp_streamed, tm=tm, ff_chunk=1024, final_norm=final_norm),
        in_specs=[pl.BlockSpec(memory_space=pl.ANY), vmem, vmem, vmem, vmem],
        out_specs=pl.BlockSpec(memory_space=pl.ANY),
        out_shape=jax.ShapeDtypeStruct(h.shape, _F32),
        compiler_params=pltpu.CompilerParams(vmem_limit_bytes=MLP_VMEM_BYTES),
        name="mlp",
    )(h, g, w_up, w_down, g_final)


def kernel(x, norm_mix, w_in, pool_w, pool_b, pool_scale, attn_sinks, p_pool, p_attn,
           w_out, norm_mlp, w_up, w_down, norm_final):
    depth = norm_mix.shape[0]
    b, s, d = x.shape
    bias = jnp.asarray(_attention_bias())
    h = x
    for l in range(depth):
        h, w_up_b, w_down_b = _mixer(
            h, attn_sinks[l], norm_mix[l][None, :], w_in[l], pool_w[l], pool_b[l].reshape(1, -1),
            pool_scale[l][None, :], bias, p_pool[l], p_attn[l], w_out[l], w_up[l], w_down[l])
        h = _mlp(h.reshape(b * s, d), norm_mlp[l][None, :], w_up_b, w_down_b,
                 norm_final[None, :], l == depth - 1).reshape(b, s, d)
    return h
```
